```python
import math
import jax, jax.numpy as jnp
from jax import lax
import numpy as np

D_MODEL = 4096
BATCH = 2
SEQ = 8192
DEPTH = 1

D_RNN = D_MODEL
LRU_BLOCK = 256
N_LRU_HEADS = D_RNN // LRU_BLOCK
RNN_CONV_WIDTH = 4
LRU_C = 8.0
D_CONV = D_MODEL
CONF_CONV_WIDTH = 31
N_BRANCHES = 2
IN_WIDTH = 2 * D_RNN + 2 * D_CONV + N_BRANCHES * D_MODEL
N_GROUPS = 4
EXPERTS_PER_GROUP = 8
N_EXPERTS = N_GROUPS * EXPERTS_PER_GROUP
TOP_K = 2
D_FF_EXPERT = D_MODEL // 4
DISPATCH_BLOCK = 128
N_MOD = 6
RMS_EPS = 1e-6
LN_EPS = 1e-5

kernel_name = "hybrid_rglru_conformer_hiermoe_adaln"


def rms_norm(x, g):
    xf = x.astype(jnp.float32)
    y = xf * lax.rsqrt(jnp.mean(xf * xf, axis=-1, keepdims=True) + RMS_EPS)
    return (y * g.astype(jnp.float32)).astype(x.dtype)


def layer_norm(x, g, b):
    xf = x.astype(jnp.float32)
    mu = jnp.mean(xf, axis=-1, keepdims=True)
    xc = xf - mu
    var = jnp.mean(xc * xc, axis=-1, keepdims=True)
    y = xc * lax.rsqrt(var + LN_EPS)
    return (y * g.astype(jnp.float32) + b.astype(jnp.float32)).astype(x.dtype)


def causal_dwconv(x, w, b):
    k = w.shape[0]
    y = lax.conv_general_dilated(
        x, w[:, None, :].astype(x.dtype), window_strides=(1,), padding=[(k - 1, 0)],
        dimension_numbers=("NWC", "WIO", "NWC"), feature_group_count=x.shape[-1])
    return y + b.astype(x.dtype)


def rg_lru(x, w_a, b_a, w_x, b_x, lam):
    bsz, seq, _ = x.shape
    xf = x.astype(jnp.float32)
    xh = xf.reshape(bsz, seq, N_LRU_HEADS, LRU_BLOCK)
    r = jax.nn.sigmoid(jnp.einsum("bshi,hij->bshj", xh, w_a.astype(jnp.float32)) + b_a.astype(jnp.float32))
    i = jax.nn.sigmoid(jnp.einsum("bshi,hij->bshj", xh, w_x.astype(jnp.float32)) + b_x.astype(jnp.float32))
    r = r.reshape(bsz, seq, D_RNN)
    i = i.reshape(bsz, seq, D_RNN)
    log_a = -LRU_C * r * jax.nn.softplus(-lam.astype(jnp.float32))
    a = jnp.exp(log_a)
    b = jnp.sqrt(-jnp.expm1(2.0 * log_a)) * (i * xf)

    def combine(left, right):
        a_l, b_l = left
        a_r, b_r = right
        return a_l * a_r, a_r * b_l + b_r

    _, h = lax.associative_scan(combine, (a, b), axis=1)
    return h.astype(x.dtype)


def mixer(h, w_in, b_in, rnn_conv_w, rnn_conv_b, lru_wa, lru_ba, lru_wx, lru_bx, lru_lambda,
          w_rnn_out, conf_dw_w, conf_dw_b, conf_ln_g, conf_ln_b, w_conv_out, w_o):
    proj = h @ w_in + b_in
    splits = np.cumsum([D_RNN, D_RNN, D_CONV, D_CONV, D_MODEL]).tolist()
    x_rnn, y_rnn, glu_val, glu_gate, g_a, g_b = jnp.split(proj, splits, axis=-1)
    xr = causal_dwconv(x_rnn, rnn_conv_w, rnn_conv_b)
    rnn = rg_lru(xr, lru_wa, lru_ba, lru_wx, lru_bx, lru_lambda) * jax.nn.gelu(y_rnn)
    branch_a = rnn @ w_rnn_out
    u = glu_val * jax.nn.sigmoid(glu_gate)
    u = causal_dwconv(u, conf_dw_w, conf_dw_b)
    u = jax.nn.silu(layer_norm(u, conf_ln_g, conf_ln_b))
    branch_b = u @ w_conv_out
    merged = jax.nn.sigmoid(g_a) * branch_a + jax.nn.sigmoid(g_b) * branch_b
    return merged @ w_o


def hier_moe(h, rg_w, rg_b, re_w, re_b, w1, w3, w2):
    bsz, seq, d = h.shape
    n_tok = bsz * seq
    ht = h.reshape(n_tok, d)
    g_logits = (ht @ rg_w).astype(jnp.float32) + rg_b.astype(jnp.float32)
    g_prob = jax.nn.softmax(g_logits, axis=-1)
    g_p, g_idx = lax.top_k(g_prob, 1)
    e_logits = ((ht @ re_w).astype(jnp.float32) + re_b.astype(jnp.float32))
    e_logits = e_logits.reshape(n_tok, N_GROUPS, EXPERTS_PER_GROUP)
    e_sel = jnp.take_along_axis(e_logits, g_idx[:, :, None], axis=1)[:, 0]
    e_prob = jax.nn.softmax(e_sel, axis=-1)
    e_p, e_idx = lax.top_k(e_prob, TOP_K)
    e_p = e_p / jnp.sum(e_p, axis=-1, keepdims=True)
    weights = g_p * e_p
    expert_id = g_idx * EXPERTS_PER_GROUP + e_idx

    n_assign = n_tok * TOP_K
    flat_e = expert_id.reshape(n_assign).astype(jnp.int32)
    flat_t = jnp.repeat(jnp.arange(n_tok, dtype=jnp.int32), TOP_K)
    flat_w = weights.reshape(n_assign)
    order = jnp.argsort(flat_e)
    e_s, t_s, w_s = flat_e[order], flat_t[order], flat_w[order]
    counts = jnp.bincount(flat_e, length=N_EXPERTS)
    starts = jnp.cumsum(counts) - counts
    padded = ((counts + DISPATCH_BLOCK - 1) // DISPATCH_BLOCK) * DISPATCH_BLOCK
    pad_ends = jnp.cumsum(padded)
    pad_starts = pad_ends - padded
    dest = pad_starts[e_s] + (jnp.arange(n_assign, dtype=jnp.int32) - starts[e_s])
    n_blocks = -(-n_assign // DISPATCH_BLOCK) + N_EXPERTS
    n_rows = n_blocks * DISPATCH_BLOCK
    tok_buf = jnp.full((n_rows,), n_tok, dtype=jnp.int32).at[dest].set(t_s)
    w_buf = jnp.zeros((n_rows,), dtype=jnp.float32).at[dest].set(w_s)
    block_start = jnp.arange(n_blocks, dtype=jnp.int32) * DISPATCH_BLOCK
    block_e = jnp.clip(jnp.searchsorted(pad_ends, block_start, side="right"), 0, N_EXPERTS - 1)

    h_pad = jnp.concatenate([ht, jnp.zeros((1, d), ht.dtype)], axis=0)
    xb = h_pad[tok_buf].reshape(n_blocks, DISPATCH_BLOCK, d)

    def expert_block(args):
        xblk, e = args
        gate = xblk @ w1[e]
        up = xblk @ w3[e]
        return (jax.nn.silu(gate) * up) @ w2[e]

    yb = lax.map(expert_block, (xb, block_e)).reshape(n_rows, d)
    yb = yb * w_buf[:, None].astype(yb.dtype)
    out = jax.ops.segment_sum(yb, tok_buf, num_segments=n_tok + 1)[:n_tok]
    return out.reshape(bsz, seq, d)


def setup_inputs(seed: int = 0) -> dict:
    key = jax.random.key(seed)
    ks = jax.random.split(key, 32)
    f32 = jnp.float32
    L = DEPTH

    def nrm(k, shape, scale):
        return jax.random.normal(k, shape, f32) * scale

    a_init = jax.random.uniform(ks[12], (L, D_RNN), f32, 0.9, 0.999)
    return {
        "x": jax.random.normal(ks[0], (BATCH, SEQ, D_MODEL), f32),
        "c": jax.random.normal(ks[1], (BATCH, D_MODEL), f32),
        "ada_w": nrm(ks[2], (L, D_MODEL, N_MOD * D_MODEL), 0.5 * D_MODEL ** -0.5),
        "ada_b": nrm(ks[3], (L, N_MOD * D_MODEL), 0.01),
        "norm_mix_g": 1.0 + nrm(ks[4], (L, D_MODEL), 0.05),
        "w_in": nrm(ks[5], (L, D_MODEL, IN_WIDTH), D_MODEL ** -0.5),
        "b_in": nrm(ks[6], (L, IN_WIDTH), 0.01),
        "rnn_conv_w": nrm(ks[7], (L, RNN_CONV_WIDTH, D_RNN), RNN_CONV_WIDTH ** -0.5),
        "rnn_conv_b": nrm(ks[8], (L, D_RNN), 0.01),
        "lru_wa": nrm(ks[9], (L, N_LRU_HEADS, LRU_BLOCK, LRU_BLOCK), LRU_BLOCK ** -0.5),
        "lru_ba": nrm(ks[10], (L, N_LRU_HEADS, LRU_BLOCK), 0.01),
        "lru_wx": nrm(ks[11], (L, N_LRU_HEADS, LRU_BLOCK, LRU_BLOCK), LRU_BLOCK ** -0.5),
        "lru_bx": nrm(ks[13], (L, N_LRU_HEADS, LRU_BLOCK), 0.01),
        "lru_lambda": jnp.log(a_init) - jnp.log1p(-a_init),
        "w_rnn_out": nrm(ks[14], (L, D_RNN, D_MODEL), D_RNN ** -0.5),
        "conf_dw_w": nrm(ks[15], (L, CONF_CONV_WIDTH, D_CONV), CONF_CONV_WIDTH ** -0.5),
        "conf_dw_b": nrm(ks[16], (L, D_CONV), 0.01),
        "conf_ln_g": 1.0 + nrm(ks[17], (L, D_CONV), 0.05),
        "conf_ln_b": nrm(ks[18], (L, D_CONV), 0.01),
        "w_conv_out": nrm(ks[19], (L, D_CONV, D_MODEL), D_CONV ** -0.5),
        "w_o": nrm(ks[20], (L, D_MODEL, D_MODEL), D_MODEL ** -0.5),
        "norm_ffn_g": 1.0 + nrm(ks[21], (L, D_MODEL), 0.05),
        "router_group_w": nrm(ks[22], (L, D_MODEL, N_GROUPS), D_MODEL ** -0.5),
        "router_group_b": nrm(ks[23], (L, N_GROUPS), 0.01),
        "router_expert_w": nrm(ks[24], (L, D_MODEL, N_EXPERTS), D_MODEL ** -0.5),
        "router_expert_b": nrm(ks[25], (L, N_EXPERTS), 0.01),
        "expert_w1": nrm(ks[26], (L, N_EXPERTS, D_MODEL, D_FF_EXPERT), D_MODEL ** -0.5),
        "expert_w3": nrm(ks[27], (L, N_EXPERTS, D_MODEL, D_FF_EXPERT), D_MODEL ** -0.5),
        "expert_w2": nrm(ks[28], (L, N_EXPERTS, D_FF_EXPERT, D_MODEL), D_FF_EXPERT ** -0.5),
        "final_norm_g": 1.0 + nrm(ks[29], (D_MODEL,), 0.05),
    }


def reference(x, c, ada_w, ada_b, norm_mix_g, w_in, b_in, rnn_conv_w, rnn_conv_b,
              lru_wa, lru_ba, lru_wx, lru_bx, lru_lambda, w_rnn_out, conf_dw_w, conf_dw_b,
              conf_ln_g, conf_ln_b, w_conv_out, w_o, norm_ffn_g, router_group_w, router_group_b,
              router_expert_w, router_expert_b, expert_w1, expert_w3, expert_w2, final_norm_g):
    c_act = jax.nn.silu(c)
    for l in range(DEPTH):
        mod = c_act @ ada_w[l] + ada_b[l]
        sh1, sc1, gt1, sh2, sc2, gt2 = jnp.split(mod[:, None, :], N_MOD, axis=-1)
        h = rms_norm(x, norm_mix_g[l]) * (1.0 + sc1) + sh1
        mix = mixer(h, w_in[l], b_in[l], rnn_conv_w[l], rnn_conv_b[l], lru_wa[l], lru_ba[l],
                    lru_wx[l], lru_bx[l], lru_lambda[l], w_rnn_out[l], conf_dw_w[l], conf_dw_b[l],
                    conf_ln_g[l], conf_ln_b[l], w_conv_out[l], w_o[l])
        x = x + gt1 * mix
        h = rms_norm(x, norm_ffn_g[l]) * (1.0 + sc2) + sh2
        ffn = hier_moe(h, router_group_w[l], router_group_b[l], router_expert_w[l],
                       router_expert_b[l], expert_w1[l], expert_w3[l], expert_w2[l])
        x = x + gt2 * ffn
    return rms_norm(x, final_norm_g)
```

```python
import functools

import jax
import jax.numpy as jnp
from jax import lax
from jax.experimental import pallas as pl
from jax.experimental.pallas import tpu as pltpu

F32 = jnp.float32
BF16 = jnp.bfloat16

RMS_EPS = 1e-6
LN_EPS = 1e-5
LRU_C = 8.0
LRU_BLOCK = 256
TOP_K = 2
SUBLANES = 8
LANES = 128
NEG_BIG = -1e30

TM_PROJ = 512
TN_PROJ = 512
TM_MERGE = 512
TN_MERGE = 512
TM_OUT = 1024
TN_OUT = 512
TM_NORM = 256
TN_ADA = 512
BM_EXPERT = 256
TM_COMBINE = 256

VMEM_LIMIT = 56 * 1024 * 1024


def _pick(pref, dim):
    t = min(pref, dim)
    while dim % t:
        t -= 1
    return t


def _params(semantics):
    return pltpu.CompilerParams(dimension_semantics=semantics, vmem_limit_bytes=VMEM_LIMIT)


def _silu(v):
    return v * jax.nn.sigmoid(v)


def _ada_kernel(c_ref, w_ref, b_ref, o_ref):
    c = c_ref[...]
    ca = _silu(c).astype(BF16)
    o_ref[...] = jnp.dot(ca, w_ref[...].astype(BF16), preferred_element_type=F32) + b_ref[...]


def _ada_mod(c, ada_w, ada_b):
    bsz, d = c.shape
    n = ada_w.shape[1]
    rows = -(-bsz // SUBLANES) * SUBLANES
    c_pad = jnp.zeros((rows, d), F32).at[:bsz].set(c)
    tn = _pick(TN_ADA, n)
    out = pl.pallas_call(
        _ada_kernel,
        out_shape=jax.ShapeDtypeStruct((rows, n), F32),
        grid=(n // tn,),
        in_specs=[pl.BlockSpec((rows, d), lambda j: (0, 0)),
                  pl.BlockSpec((d, tn), lambda j: (0, j)),
                  pl.BlockSpec((1, tn), lambda j: (0, j))],
        out_specs=pl.BlockSpec((rows, tn), lambda j: (0, j)),
        compiler_params=_params(("arbitrary",)),
        name="ada_mod",
    )(c_pad, ada_w, ada_b.reshape(1, n))
    return out[:bsz]


def _norm_mod_kernel(x_ref, g_ref, mod_ref, o_ref, *, shift_idx, scale_idx):
    x = x_ref[...]
    ms = jnp.mean(x * x, axis=-1, keepdims=True)
    y = x * lax.rsqrt(ms + RMS_EPS) * g_ref[...]
    scale = mod_ref[0, scale_idx:scale_idx + 1, :]
    shift = mod_ref[0, shift_idx:shift_idx + 1, :]
    o_ref[...] = (y * (1.0 + scale) + shift).astype(o_ref.dtype)


def _norm_mod(x2d, g, mod3, seq, shift_idx, scale_idx):
    n, d = x2d.shape
    tm = _pick(TM_NORM, seq)
    tpb = seq // tm
    n_mod = mod3.shape[1]
    return pl.pallas_call(
        functools.partial(_norm_mod_kernel, shift_idx=shift_idx, scale_idx=scale_idx),
        out_shape=jax.ShapeDtypeStruct((n, d), BF16),
        grid=(n // tm,),
        in_specs=[pl.BlockSpec((tm, d), lambda i: (i, 0)),
                  pl.BlockSpec((1, d), lambda i: (0, 0)),
                  pl.BlockSpec((1, n_mod, d), lambda i: (i // tpb, 0, 0))],
        out_specs=pl.BlockSpec((tm, d), lambda i: (i, 0)),
        compiler_params=_params(("arbitrary",)),
        name="norm_mod",
    )(x2d, g.reshape(1, d), mod3)


def _rnn_kernel(h_ref, wx_ref, wy_ref, bx_ref, by_ref, cw_ref, cb_ref, wa_ref, ba_ref,
                wi_ref, bi_ref, lam_ref, o_ref, xbuf, abuf, bbuf, gbuf, hcar,
                *, tm, tn, tpb, kw):
    i = pl.program_id(1)

    @pl.when(i % tpb == 0)
    def _():
        xbuf[0:SUBLANES, :] = jnp.zeros((SUBLANES, tn), F32)
        hcar[...] = jnp.zeros_like(hcar)

    h = h_ref[...]
    xr = jnp.dot(h, wx_ref[...], preferred_element_type=F32) + bx_ref[...]
    y = jnp.dot(h, wy_ref[...], preferred_element_type=F32) + by_ref[...]
    gbuf[...] = jax.nn.gelu(y)

    xbuf[SUBLANES:SUBLANES + tm, :] = xr
    xc = cb_ref[...] + cw_ref[kw - 1:kw, :] * xr
    for k in range(kw - 1):
        off = SUBLANES - (kw - 1) + k
        xc = xc + cw_ref[k:k + 1, :] * xbuf[off:off + tm, :]
    xbuf[0:SUBLANES, :] = xbuf[tm:tm + SUBLANES, :]

    sp = jax.nn.softplus(-lam_ref[...])
    xcb = xc.astype(BF16)
    for hh in range(tn // LRU_BLOCK):
        sl = slice(hh * LRU_BLOCK, (hh + 1) * LRU_BLOCK)
        xs = xcb[:, sl]
        r = jax.nn.sigmoid(jnp.dot(xs, wa_ref[hh], preferred_element_type=F32) + ba_ref[:, sl])
        ig = jax.nn.sigmoid(jnp.dot(xs, wi_ref[hh], preferred_element_type=F32) + bi_ref[:, sl])
        log_a = (-LRU_C) * r * sp[:, sl]
        a = jnp.exp(log_a)
        abuf[:, sl] = a
        one_minus_a2 = -jnp.tanh(log_a) * (a * a + 1.0)
        bbuf[:, sl] = jnp.sqrt(one_minus_a2) * (ig * xc[:, sl])

    row = lax.broadcasted_iota(jnp.int32, (SUBLANES, tn), 0)

    def group(g, hprev):
        r0 = pl.multiple_of(g * SUBLANES, SUBLANES)
        a = abuf[pl.ds(r0, SUBLANES), :]
        b = bbuf[pl.ds(r0, SUBLANES), :]
        for d in (1, 2, 4):
            keep = row >= d
            a_s = jnp.where(keep, pltpu.roll(a, d, 0), 1.0)
            b_s = jnp.where(keep, pltpu.roll(b, d, 0), 0.0)
            b = a * b_s + b
            a = a * a_s
        hg = a * hprev + b
        bbuf[pl.ds(r0, SUBLANES), :] = hg
        return hg[SUBLANES - 1:SUBLANES, :]

    hlast = lax.fori_loop(0, tm // SUBLANES, group, hcar[0:1, :])
    hcar[0:1, :] = hlast
    o_ref[...] = (bbuf[...] * gbuf[...]).astype(o_ref.dtype)


def _rnn_branch(h, w_in, b_in2, conv_w, conv_b, wa, ba, wi, bi, lam, seq, d_rnn):
    n, d = h.shape
    tm = _pick(TM_PROJ, seq)
    tn = _pick(TN_PROJ, d_rnn)
    assert tn % LRU_BLOCK == 0 and tm % SUBLANES == 0
    tpb = seq // tm
    kw = conv_w.shape[0]
    assert kw - 1 <= SUBLANES
    nj = d_rnn // tn
    hpt = tn // LRU_BLOCK
    vec = lambda off: pl.BlockSpec((1, tn), lambda j, i: (0, off + j))
    return pl.pallas_call(
        functools.partial(_rnn_kernel, tm=tm, tn=tn, tpb=tpb, kw=kw),
        out_shape=jax.ShapeDtypeStruct((n, d_rnn), BF16),
        grid=(nj, n // tm),
        in_specs=[pl.BlockSpec((tm, d), lambda j, i: (i, 0)),
                  pl.BlockSpec((d, tn), lambda j, i: (0, j)),
                  pl.BlockSpec((d, tn), lambda j, i: (0, nj + j)),
                  vec(0), vec(nj),
                  pl.BlockSpec((kw, tn), lambda j, i: (0, j)),
                  vec(0),
                  pl.BlockSpec((hpt, LRU_BLOCK, LRU_BLOCK), lambda j, i: (j, 0, 0)),
                  vec(0),
                  pl.BlockSpec((hpt, LRU_BLOCK, LRU_BLOCK), lambda j, i: (j, 0, 0)),
                  vec(0), vec(0)],
        out_specs=pl.BlockSpec((tm, tn), lambda j, i: (i, j)),
        scratch_shapes=[pltpu.VMEM((tm + SUBLANES, tn), F32),
                        pltpu.VMEM((tm, tn), F32),
                        pltpu.VMEM((tm, tn), F32),
                        pltpu.VMEM((tm, tn), F32),
                        pltpu.VMEM((SUBLANES, tn), F32)],
        compiler_params=_params(("arbitrary", "arbitrary")),
        name="rnn_branch",
    )(h, w_in, w_in, b_in2, b_in2, conv_w, conv_b.reshape(1, -1), wa, ba.reshape(1, -1),
      wi, bi.reshape(1, -1), lam.reshape(1, -1))


CONV_ROWS = 32


def _conv_kernel(h_ref, wv_ref, wg_ref, bv_ref, bg_ref, cw_ref, cb_ref, o_ref, ubuf,
                 *, tm, tn, tpb, kw, halo):
    i = pl.program_id(1)

    @pl.when(i % tpb == 0)
    def _():
        ubuf[0:halo, :] = jnp.zeros((halo, tn), F32)

    h = h_ref[...]
    val = jnp.dot(h, wv_ref[...], preferred_element_type=F32) + bv_ref[...]
    gate = jnp.dot(h, wg_ref[...], preferred_element_type=F32) + bg_ref[...]
    ubuf[halo:halo + tm, :] = val * jax.nn.sigmoid(gate)

    for c in range(tn // LANES):
        cs = slice(c * LANES, (c + 1) * LANES)

        def chunk(qi, carry):
            r0 = pl.multiple_of(qi * CONV_ROWS, CONV_ROWS)
            win = ubuf[pl.ds(r0, CONV_ROWS + halo), cs]
            acc = jnp.broadcast_to(cb_ref[:, cs], (CONV_ROWS, LANES))
            for r in range(SUBLANES):
                v = win if r == 0 else pltpu.roll(win, r, 0)
                for q in range(halo // SUBLANES):
                    s = SUBLANES * q + r
                    if s <= kw - 1:
                        lo = halo - SUBLANES * q
                        acc = acc + cw_ref[kw - 1 - s:kw - s, cs] * v[lo:lo + CONV_ROWS, :]
            o_ref[pl.ds(r0, CONV_ROWS), cs] = acc
            return carry

        lax.fori_loop(0, tm // CONV_ROWS, chunk, 0)

    ubuf[0:halo, :] = ubuf[tm:tm + halo, :]


def _conv_branch(h, w_in, b_in2, conv_w, conv_b, seq, d_conv, col0):
    n, d = h.shape
    tm = _pick(TM_PROJ, seq)
    tn = _pick(TN_PROJ, d_conv)
    tpb = seq // tm
    kw = conv_w.shape[0]
    halo = -(-(kw - 1) // SUBLANES) * SUBLANES
    assert tm % CONV_ROWS == 0 and tn % LANES == 0 and halo <= tm
    nj = d_conv // tn
    vec = lambda off: pl.BlockSpec((1, tn), lambda j, i: (0, off + j))
    return pl.pallas_call(
        functools.partial(_conv_kernel, tm=tm, tn=tn, tpb=tpb, kw=kw, halo=halo),
        out_shape=jax.ShapeDtypeStruct((n, d_conv), F32),
        grid=(nj, n // tm),
        in_specs=[pl.BlockSpec((tm, d), lambda j, i: (i, 0)),
                  pl.BlockSpec((d, tn), lambda j, i: (0, col0 * nj + j)),
                  pl.BlockSpec((d, tn), lambda j, i: (0, (col0 + 1) * nj + j)),
                  vec(col0 * nj), vec((col0 + 1) * nj),
                  pl.BlockSpec((kw, tn), lambda j, i: (0, j)),
                  vec(0)],
        out_specs=pl.BlockSpec((tm, tn), lambda j, i: (i, j)),
        scratch_shapes=[pltpu.VMEM((tm + halo, tn), F32)],
        compiler_params=_params(("arbitrary", "arbitrary")),
        name="conv_branch",
    )(h, w_in, w_in, b_in2, b_in2, conv_w, conv_b.reshape(1, -1))


def _gate_kernel(h_ref, wa_ref, wb_ref, ba_ref, bb_ref, oa_ref, ob_ref):
    h = h_ref[...]
    ga = jnp.dot(h, wa_ref[...], preferred_element_type=F32) + ba_ref[...]
    gb = jnp.dot(h, wb_ref[...], preferred_element_type=F32) + bb_ref[...]
    oa_ref[...] = jax.nn.sigmoid(ga).astype(oa_ref.dtype)
    ob_ref[...] = jax.nn.sigmoid(gb).astype(ob_ref.dtype)


def _gate_proj(h, w_in, b_in2, seq, d_model, col0):
    n, d = h.shape
    tm = _pick(TM_PROJ, seq)
    tn = _pick(TN_PROJ, d_model)
    nj = d_model // tn
    vec = lambda off: pl.BlockSpec((1, tn), lambda j, i: (0, off + j))
    out = jax.ShapeDtypeStruct((n, d_model), BF16)
    return pl.pallas_call(
        _gate_kernel,
        out_shape=(out, out),
        grid=(nj, n // tm),
        in_specs=[pl.BlockSpec((tm, d), lambda j, i: (i, 0)),
                  pl.BlockSpec((d, tn), lambda j, i: (0, col0 * nj + j)),
                  pl.BlockSpec((d, tn), lambda j, i: (0, (col0 + 1) * nj + j)),
                  vec(col0 * nj), vec((col0 + 1) * nj)],
        out_specs=(pl.BlockSpec((tm, tn), lambda j, i: (i, j)),
                   pl.BlockSpec((tm, tn), lambda j, i: (i, j))),
        compiler_params=_params(("arbitrary", "arbitrary")),
        name="gate_proj",
    )(h, w_in, w_in, b_in2, b_in2)


LN_ROWS = 32


def _merge_kernel(rg_ref, uc_ref, sa_ref, sb_ref, wr_ref, wc_ref, lg_ref, lb_ref, o_ref, ub):
    @pl.when(pl.program_id(1) == 0)
    def _():
        def rows(q, carry):
            r0 = pl.multiple_of(q * LN_ROWS, LN_ROWS)
            u = uc_ref[pl.ds(r0, LN_ROWS), :]
            mu = jnp.mean(u, axis=-1, keepdims=True)
            xc = u - mu
            var = jnp.mean(xc * xc, axis=-1, keepdims=True)
            y = xc * lax.rsqrt(var + LN_EPS) * lg_ref[...] + lb_ref[...]
            ub[pl.ds(r0, LN_ROWS), :] = _silu(y).astype(ub.dtype)
            return carry

        lax.fori_loop(0, uc_ref.shape[0] // LN_ROWS, rows, 0)

    a = jnp.dot(rg_ref[...], wr_ref[...], preferred_element_type=F32)
    b = jnp.dot(ub[...], wc_ref[...], preferred_element_type=F32)
    o_ref[...] = (sa_ref[...].astype(F32) * a + sb_ref[...].astype(F32) * b).astype(o_ref.dtype)


def _merge(rnn_g, uc, sa, sb, w_rnn_out, w_conv_out, ln_g, ln_b):
    n, d_rnn = rnn_g.shape
    d_conv = uc.shape[1]
    d_model = w_rnn_out.shape[1]
    tm = _pick(TM_MERGE, n)
    tn = _pick(TN_MERGE, d_model)
    assert tm % LN_ROWS == 0
    return pl.pallas_call(
        _merge_kernel,
        out_shape=jax.ShapeDtypeStruct((n, d_model), BF16),
        grid=(n // tm, d_model // tn),
        in_specs=[pl.BlockSpec((tm, d_rnn), lambda i, j: (i, 0)),
                  pl.BlockSpec((tm, d_conv), lambda i, j: (i, 0)),
                  pl.BlockSpec((tm, tn), lambda i, j: (i, j)),
                  pl.BlockSpec((tm, tn), lambda i, j: (i, j)),
                  pl.BlockSpec((d_rnn, tn), lambda i, j: (0, j)),
                  pl.BlockSpec((d_conv, tn), lambda i, j: (0, j)),
                  pl.BlockSpec((1, d_conv), lambda i, j: (0, 0)),
                  pl.BlockSpec((1, d_conv), lambda i, j: (0, 0))],
        out_specs=pl.BlockSpec((tm, tn), lambda i, j: (i, j)),
        scratch_shapes=[pltpu.VMEM((tm, d_conv), BF16)],
        compiler_params=_params(("arbitrary", "arbitrary")),
        name="merge",
    )(rnn_g, uc, sa, sb, w_rnn_out, w_conv_out, ln_g.reshape(1, -1), ln_b.reshape(1, -1))


def _out_kernel(m_ref, w_ref, x_ref, mod_ref, o_ref, *, gate_idx):
    mix = jnp.dot(m_ref[...], w_ref[...], preferred_element_type=F32)
    o_ref[...] = x_ref[...] + mod_ref[0, gate_idx:gate_idx + 1, :] * mix


def _out_proj(merged, w_o, x2d, mod3, seq, gate_idx):
    n, d = merged.shape
    d_model = w_o.shape[1]
    tm = _pick(TM_OUT, seq)
    tn = _pick(TN_OUT, d_model)
    tpb = seq // tm
    n_mod = mod3.shape[1]
    return pl.pallas_call(
        functools.partial(_out_kernel, gate_idx=gate_idx),
        out_shape=jax.ShapeDtypeStruct((n, d_model), F32),
        grid=(n // tm, d_model // tn),
        in_specs=[pl.BlockSpec((tm, d), lambda i, j: (i, 0)),
                  pl.BlockSpec((d, tn), lambda i, j: (0, j)),
                  pl.BlockSpec((tm, tn), lambda i, j: (i, j)),
                  pl.BlockSpec((1, n_mod, tn), lambda i, j: (i // tpb, 0, j))],
        out_specs=pl.BlockSpec((tm, tn), lambda i, j: (i, j)),
        compiler_params=_params(("arbitrary", "arbitrary")),
        name="out_proj",
    )(merged, w_o, x2d, mod3)


def _norm_route_kernel(x_ref, g_ref, mod_ref, rw_ref, rb_ref, h_ref, route_ref,
                       *, shift_idx, scale_idx, n_groups, epg):
    x = x_ref[...]
    ms = jnp.mean(x * x, axis=-1, keepdims=True)
    y = x * lax.rsqrt(ms + RMS_EPS) * g_ref[...]
    h = y * (1.0 + mod_ref[0, scale_idx:scale_idx + 1, :]) + mod_ref[0, shift_idx:shift_idx + 1, :]
    h_ref[...] = h

    n_exp = n_groups * epg
    logits = jnp.dot(h, rw_ref[...], preferred_element_type=F32,
                     precision=lax.Precision.HIGHEST) + rb_ref[...]
    lane = lax.broadcasted_iota(jnp.int32, logits.shape, 1)
    lane_f = lane.astype(F32)
    big = float(4 * LANES)

    def first_argmax(v, vmax):
        return jnp.min(jnp.where(v == vmax, lane_f, big), axis=-1, keepdims=True)

    is_g = (lane >= n_exp) & (lane < n_exp + n_groups)
    gl = jnp.where(is_g, logits, NEG_BIG)
    gmax = jnp.max(gl, axis=-1, keepdims=True)
    gsum = jnp.sum(jnp.where(is_g, jnp.exp(gl - gmax), 0.0), axis=-1, keepdims=True)
    g_p = 1.0 / gsum
    g_idx = first_argmax(gl, gmax) - float(n_exp)

    lane_grp = jnp.zeros_like(lane_f)
    for g in range(1, n_groups):
        lane_grp = lane_grp + jnp.where(lane >= g * epg, 1.0, 0.0)
    in_grp = (lane < n_exp) & (lane_grp == g_idx)
    el = jnp.where(in_grp, logits, NEG_BIG)
    m1 = jnp.max(el, axis=-1, keepdims=True)
    i1 = first_argmax(el, m1)
    el2 = jnp.where(lane_f == i1, NEG_BIG, el)
    m2 = jnp.max(el2, axis=-1, keepdims=True)
    i2 = first_argmax(el2, m2)
    t = jnp.exp(m2 - m1)
    p1 = 1.0 / (1.0 + t)
    p2 = t * p1
    route = jnp.where(lane == 0, i1, 0.0)
    route = jnp.where(lane == 1, i2, route)
    route = jnp.where(lane == 2, g_p * p1, route)
    route = jnp.where(lane == 3, g_p * p2, route)
    route_ref[...] = route


def _norm_route(x1, g, mod3, rw, rb, seq, shift_idx, scale_idx, n_groups, epg):
    n, d = x1.shape
    tm = _pick(TM_NORM, seq)
    tpb = seq // tm
    n_mod = mod3.shape[1]
    return pl.pallas_call(
        functools.partial(_norm_route_kernel, shift_idx=shift_idx, scale_idx=scale_idx,
                          n_groups=n_groups, epg=epg),
        out_shape=(jax.ShapeDtypeStruct((n, d), F32), jax.ShapeDtypeStruct((n, LANES), F32)),
        grid=(n // tm,),
        in_specs=[pl.BlockSpec((tm, d), lambda i: (i, 0)),
                  pl.BlockSpec((1, d), lambda i: (0, 0)),
                  pl.BlockSpec((1, n_mod, d), lambda i: (i // tpb, 0, 0)),
                  pl.BlockSpec((d, LANES), lambda i: (0, 0)),
                  pl.BlockSpec((1, LANES), lambda i: (0, 0))],
        out_specs=(pl.BlockSpec((tm, d), lambda i: (i, 0)),
                   pl.BlockSpec((tm, LANES), lambda i: (i, 0))),
        compiler_params=_params(("arbitrary",)),
        name="norm_route",
    )(x1, g.reshape(1, d), mod3, rw, rb)


def _expert_up_kernel(be_ref, nused_ref, tok_ref, h_hbm, w1_ref, w3_ref, act_ref, xbuf, sem, *, bm):
    b = pl.program_id(0)

    @pl.when(b < nused_ref[0])
    def _():
        def issue(r, carry):
            t = tok_ref[b * bm + r]
            pltpu.make_async_copy(h_hbm.at[pl.ds(t, 1), :], xbuf.at[pl.ds(r, 1), :], sem).start()
            return carry

        lax.fori_loop(0, bm, issue, 0)
        pltpu.make_async_copy(h_hbm.at[pl.ds(0, bm), :], xbuf, sem).wait()
        x = xbuf[...].astype(BF16)
        gate = jnp.dot(x, w1_ref[0], preferred_element_type=F32)
        up = jnp.dot(x, w3_ref[0], preferred_element_type=F32)
        act_ref[...] = (_silu(gate) * up).astype(act_ref.dtype)

    @pl.when(b >= nused_ref[0])
    def _():
        act_ref[...] = jnp.zeros_like(act_ref)


def _expert_up(block_e, n_used, tok_buf, h2, w1, w3, bm):
    n_blocks = block_e.shape[0]
    d = h2.shape[1]
    f = w1.shape[2]
    grid_spec = pltpu.PrefetchScalarGridSpec(
        num_scalar_prefetch=3,
        grid=(n_blocks,),
        in_specs=[pl.BlockSpec(memory_space=pl.ANY),
                  pl.BlockSpec((1, d, f), lambda b, be, nu, tok: (be[b], 0, 0)),
                  pl.BlockSpec((1, d, f), lambda b, be, nu, tok: (be[b], 0, 0))],
        out_specs=pl.BlockSpec((bm, f), lambda b, be, nu, tok: (b, 0)),
        scratch_shapes=[pltpu.VMEM((bm, d), F32), pltpu.SemaphoreType.DMA(())],
    )
    return pl.pallas_call(
        functools.partial(_expert_up_kernel, bm=bm),
        out_shape=jax.ShapeDtypeStruct((n_blocks * bm, f), BF16),
        grid_spec=grid_spec,
        compiler_params=_params(("arbitrary",)),
        name="expert_up",
    )(block_e, n_used, tok_buf, h2, w1, w3)


def _expert_down_kernel(be_ref, nused_ref, act_ref, w2_ref, y_ref):
    b = pl.program_id(0)

    @pl.when(b < nused_ref[0])
    def _():
        y_ref[...] = jnp.dot(act_ref[...], w2_ref[0], preferred_element_type=F32)

    @pl.when(b >= nused_ref[0])
    def _():
        y_ref[...] = jnp.zeros_like(y_ref)


def _expert_down(block_e, n_used, act, w2, bm):
    n_blocks = block_e.shape[0]
    f, d = w2.shape[1], w2.shape[2]
    grid_spec = pltpu.PrefetchScalarGridSpec(
        num_scalar_prefetch=2,
        grid=(n_blocks,),
        in_specs=[pl.BlockSpec((bm, f), lambda b, be, nu: (b, 0)),
                  pl.BlockSpec((1, f, d), lambda b, be, nu: (be[b], 0, 0))],
        out_specs=pl.BlockSpec((bm, d), lambda b, be, nu: (b, 0)),
    )
    return pl.pallas_call(
        _expert_down_kernel,
        out_shape=jax.ShapeDtypeStruct((n_blocks * bm, d), F32),
        grid_spec=grid_spec,
        compiler_params=_params(("arbitrary",)),
        name="expert_down",
    )(block_e, n_used, act, w2)


def _combine_kernel(d0_ref, d1_ref, y_hbm, x_ref, route_ref, mod_ref, g_ref, o_ref,
                    y0buf, y1buf, sem0, sem1, *, tm, gate_idx):
    i = pl.program_id(0)

    def issue(r, carry):
        t = i * tm + r
        pltpu.make_async_copy(y_hbm.at[pl.ds(d0_ref[t], 1), :], y0buf.at[pl.ds(r, 1), :], sem0).start()
        pltpu.make_async_copy(y_hbm.at[pl.ds(d1_ref[t], 1), :], y1buf.at[pl.ds(r, 1), :], sem1).start()
        return carry

    lax.fori_loop(0, tm, issue, 0)
    pltpu.make_async_copy(y_hbm.at[pl.ds(0, tm), :], y0buf, sem0).wait()
    pltpu.make_async_copy(y_hbm.at[pl.ds(0, tm), :], y1buf, sem1).wait()
    route = route_ref[...]
    ffn = route[:, 2:3] * y0buf[...] + route[:, 3:4] * y1buf[...]
    x = x_ref[...] + mod_ref[0, gate_idx:gate_idx + 1, :] * ffn
    ms = jnp.mean(x * x, axis=-1, keepdims=True)
    o_ref[...] = x * lax.rsqrt(ms + RMS_EPS) * g_ref[...]


def _combine(d0, d1, yb, x1, route, mod3, g, seq, gate_idx):
    n, d = x1.shape
    tm = _pick(TM_COMBINE, seq)
    tpb = seq // tm
    n_mod = mod3.shape[1]
    grid_spec = pltpu.PrefetchScalarGridSpec(
        num_scalar_prefetch=2,
        grid=(n // tm,),
        in_specs=[pl.BlockSpec(memory_space=pl.ANY),
                  pl.BlockSpec((tm, d), lambda i, a, b: (i, 0)),
                  pl.BlockSpec((tm, LANES), lambda i, a, b: (i, 0)),
                  pl.BlockSpec((1, n_mod, d), lambda i, a, b: (i // tpb, 0, 0)),
                  pl.BlockSpec((1, d), lambda i, a, b: (0, 0))],
        out_specs=pl.BlockSpec((tm, d), lambda i, a, b: (i, 0)),
        scratch_shapes=[pltpu.VMEM((tm, d), F32), pltpu.VMEM((tm, d), F32),
                        pltpu.SemaphoreType.DMA(()), pltpu.SemaphoreType.DMA(())],
    )
    return pl.pallas_call(
        functools.partial(_combine_kernel, tm=tm, gate_idx=gate_idx),
        out_shape=jax.ShapeDtypeStruct((n, d), F32),
        grid_spec=grid_spec,
        compiler_params=_params(("arbitrary",)),
        name="combine",
    )(d0, d1, yb, x1, route, mod3, g.reshape(1, d))


def _dispatch_plan(e0, e1, n_exp, bm):
    n_tok = e0.shape[0]
    n_assign = n_tok * TOP_K
    flat_e = jnp.stack([e0, e1], axis=1).reshape(n_assign)
    counts = jnp.zeros((n_exp,), jnp.int32).at[flat_e].add(1)
    starts = jnp.cumsum(counts) - counts
    padded = ((counts + bm - 1) // bm) * bm
    pad_ends = jnp.cumsum(padded)
    pad_starts = pad_ends - padded
    order = jnp.argsort(flat_e)
    e_s = flat_e[order]
    dest_sorted = pad_starts[e_s] + (jnp.arange(n_assign, dtype=jnp.int32) - starts[e_s])
    n_blocks = -(-n_assign // bm) + n_exp
    tok_buf = jnp.zeros((n_blocks * bm,), jnp.int32).at[dest_sorted].set((order // TOP_K).astype(jnp.int32))
    dest = jnp.zeros((n_assign,), jnp.int32).at[order].set(dest_sorted).reshape(n_tok, TOP_K)
    n_used = (pad_ends[-1] // bm).astype(jnp.int32)
    blk = jnp.minimum(jnp.arange(n_blocks, dtype=jnp.int32), n_used - 1) * bm
    block_e = jnp.clip(jnp.searchsorted(pad_ends, blk, side="right"), 0, n_exp - 1).astype(jnp.int32)
    return tok_buf, dest[:, 0], dest[:, 1], block_e, n_used.reshape(1)


def kernel(x, c, ada_w, ada_b, norm_mix_g, w_in, b_in, rnn_conv_w, rnn_conv_b, lru_wa, lru_ba, lru_wx, lru_bx, lru_lambda, w_rnn_out, conf_dw_w, conf_dw_b, conf_ln_g, conf_ln_b, w_conv_out, w_o, norm_ffn_g, router_group_w, router_group_b, router_expert_w, router_expert_b, expert_w1, expert_w3, expert_w2, final_norm_g):
    bsz, seq, d = x.shape
    depth = ada_w.shape[0]
    n_mod = ada_w.shape[2] // d
    d_rnn = lru_lambda.shape[1]
    d_conv = conf_dw_b.shape[1]
    n_groups = router_group_w.shape[2]
    n_exp = router_expert_w.shape[2]
    epg = n_exp // n_groups
    assert depth == 1 and d_rnn == d and d_conv == d and n_exp + n_groups <= LANES
    n_tok = bsz * seq
    x2d = x.reshape(n_tok, d)

    for l in range(depth):
        mod = _ada_mod(c, ada_w[l], ada_b[l])
        mod3 = mod.reshape(bsz, n_mod, d)
        w_in_b = w_in[l].astype(BF16)
        b_in2 = b_in[l].reshape(1, -1)

        h = _norm_mod(x2d, norm_mix_g[l], mod3, seq, 0, 1)
        rnn_g = _rnn_branch(h, w_in_b, b_in2, rnn_conv_w[l], rnn_conv_b[l],
                            lru_wa[l].astype(BF16), lru_ba[l], lru_wx[l].astype(BF16), lru_bx[l],
                            lru_lambda[l], seq, d_rnn)
        uc = _conv_branch(h, w_in_b, b_in2, conf_dw_w[l], conf_dw_b[l], seq, d_conv, 2)
        sa, sb = _gate_proj(h, w_in_b, b_in2, seq, d, 4)
        merged = _merge(rnn_g, uc, sa, sb, w_rnn_out[l].astype(BF16), w_conv_out[l].astype(BF16),
                        conf_ln_g[l], conf_ln_b[l])
        x1 = _out_proj(merged, w_o[l].astype(BF16), x2d, mod3, seq, 2)

        rw = jnp.zeros((d, LANES), F32).at[:, :n_exp].set(router_expert_w[l])
        rw = rw.at[:, n_exp:n_exp + n_groups].set(router_group_w[l])
        rb = jnp.zeros((1, LANES), F32).at[0, :n_exp].set(router_expert_b[l])
        rb = rb.at[0, n_exp:n_exp + n_groups].set(router_group_b[l])
        h2, route = _norm_route(x1, norm_ffn_g[l], mod3, rw, rb, seq, 3, 4, n_groups, epg)

        bm = BM_EXPERT
        e0 = route[:, 0].astype(jnp.int32)
        e1 = route[:, 1].astype(jnp.int32)
        tok_buf, d0, d1, block_e, n_used = _dispatch_plan(e0, e1, n_exp, bm)
        act = _expert_up(block_e, n_used, tok_buf, h2, expert_w1[l].astype(BF16),
                         expert_w3[l].astype(BF16), bm)
        yb = _expert_down(block_e, n_used, act, expert_w2[l].astype(BF16), bm)
        x2d = _combine(d0, d1, yb, x1, route, mod3, final_norm_g, seq, 5)

    return x2d.reshape(bsz, seq, d)
```

```python
import functools

import jax
import jax.numpy as jnp
from jax import lax
from jax.experimental import pallas as pl
from jax.experimental.pallas import tpu as pltpu

F32 = jnp.float32
BF16 = jnp.bfloat16

RMS_EPS = 1e-6
LN_EPS = 1e-5
LRU_C = 8.0
LRU_BLOCK = 256
TOP_K = 2
SUBLANES = 8
LANES = 128
NEG_BIG = -1e30

TM_PROJ = 512
TN_PROJ = 512
TM_GATE = 1024
TM_MERGE = 512
TN_MERGE = 512
TM_OUT = 1024
TN_OUT = 512
TM_NORM = 256
TN_ADA = 512
TM_CAST = 512
BM_EXPERT = 256
TM_COMBINE = 256
DMA_UNROLL = 8

VMEM_LIMIT = 56 * 1024 * 1024


def _pick(pref, dim):
    t = min(pref, dim)
    while dim % t:
        t -= 1
    return t


def _params(semantics):
    return pltpu.CompilerParams(dimension_semantics=semantics, vmem_limit_bytes=VMEM_LIMIT)


def _silu(v):
    return v * jax.nn.sigmoid(v)


def _ada_kernel(c_ref, w_ref, b_ref, o_ref):
    c = c_ref[...]
    ca = _silu(c).astype(BF16)
    o_ref[...] = jnp.dot(ca, w_ref[...].astype(BF16), preferred_element_type=F32) + b_ref[...]


def _ada_mod(c, ada_w, ada_b):
    bsz, d = c.shape
    n = ada_w.shape[1]
    rows = -(-bsz // SUBLANES) * SUBLANES
    c_pad = jnp.zeros((rows, d), F32).at[:bsz].set(c)
    tn = _pick(TN_ADA, n)
    out = pl.pallas_call(
        _ada_kernel,
        out_shape=jax.ShapeDtypeStruct((rows, n), F32),
        grid=(n // tn,),
        in_specs=[pl.BlockSpec((rows, d), lambda j: (0, 0)),
                  pl.BlockSpec((d, tn), lambda j: (0, j)),
                  pl.BlockSpec((1, tn), lambda j: (0, j))],
        out_specs=pl.BlockSpec((rows, tn), lambda j: (0, j)),
        compiler_params=_params(("arbitrary",)),
        name="ada_mod",
    )(c_pad, ada_w, ada_b.reshape(1, n))
    return out[:bsz]


def _cast_kernel(w_ref, o_ref):
    o_ref[...] = w_ref[...].astype(o_ref.dtype)


def _cast_bf16(w):
    k, n = w.shape
    tk = _pick(TM_CAST, k)
    tn = _pick(4096, n)
    return pl.pallas_call(
        _cast_kernel,
        out_shape=jax.ShapeDtypeStruct((k, n), BF16),
        grid=(k // tk, n // tn),
        in_specs=[pl.BlockSpec((tk, tn), lambda i, j: (i, j))],
        out_specs=pl.BlockSpec((tk, tn), lambda i, j: (i, j)),
        compiler_params=_params(("arbitrary", "arbitrary")),
        name="cast_bf16",
    )(w)


def _norm_mod_kernel(x_ref, g_ref, mod_ref, o_ref, *, shift_idx, scale_idx):
    x = x_ref[...]
    ms = jnp.mean(x * x, axis=-1, keepdims=True)
    y = x * lax.rsqrt(ms + RMS_EPS) * g_ref[...]
    scale = mod_ref[0, scale_idx:scale_idx + 1, :]
    shift = mod_ref[0, shift_idx:shift_idx + 1, :]
    o_ref[...] = (y * (1.0 + scale) + shift).astype(o_ref.dtype)


def _norm_mod(x2d, g, mod3, seq, shift_idx, scale_idx):
    n, d = x2d.shape
    tm = _pick(TM_NORM, seq)
    tpb = seq // tm
    n_mod = mod3.shape[1]
    return pl.pallas_call(
        functools.partial(_norm_mod_kernel, shift_idx=shift_idx, scale_idx=scale_idx),
        out_shape=jax.ShapeDtypeStruct((n, d), BF16),
        grid=(n // tm,),
        in_specs=[pl.BlockSpec((tm, d), lambda i: (i, 0)),
                  pl.BlockSpec((1, d), lambda i: (0, 0)),
                  pl.BlockSpec((1, n_mod, d), lambda i: (i // tpb, 0, 0))],
        out_specs=pl.BlockSpec((tm, d), lambda i: (i, 0)),
        compiler_params=_params(("arbitrary",)),
        name="norm_mod",
    )(x2d, g.reshape(1, d), mod3)


def _rnn_kernel(h_ref, wx_ref, wy_ref, bx_ref, by_ref, cw_ref, cb_ref, wa_ref, ba_ref,
                wi_ref, bi_ref, lam_ref, o_ref, xbuf, abuf, bbuf, gbuf, hcar,
                *, tm, tn, tpb, kw):
    i = pl.program_id(1)

    @pl.when(i % tpb == 0)
    def _():
        xbuf[0:SUBLANES, :] = jnp.zeros((SUBLANES, tn), F32)
        hcar[...] = jnp.zeros_like(hcar)

    h = h_ref[...]
    xr = jnp.dot(h, wx_ref[...], preferred_element_type=F32) + bx_ref[...]
    y = jnp.dot(h, wy_ref[...], preferred_element_type=F32) + by_ref[...]
    gbuf[...] = jax.nn.gelu(y)

    xbuf[SUBLANES:SUBLANES + tm, :] = xr
    xc = cb_ref[...] + cw_ref[kw - 1:kw, :] * xr
    for k in range(kw - 1):
        off = SUBLANES - (kw - 1) + k
        xc = xc + cw_ref[k:k + 1, :] * xbuf[off:off + tm, :]
    xbuf[0:SUBLANES, :] = xbuf[tm:tm + SUBLANES, :]

    sp = jax.nn.softplus(-lam_ref[...])
    xcb = xc.astype(BF16)
    for hh in range(tn // LRU_BLOCK):
        sl = slice(hh * LRU_BLOCK, (hh + 1) * LRU_BLOCK)
        xs = xcb[:, sl]
        r = jax.nn.sigmoid(jnp.dot(xs, wa_ref[hh], preferred_element_type=F32) + ba_ref[:, sl])
        ig = jax.nn.sigmoid(jnp.dot(xs, wi_ref[hh], preferred_element_type=F32) + bi_ref[:, sl])
        log_a = (-LRU_C) * r * sp[:, sl]
        a = jnp.exp(log_a)
        abuf[:, sl] = a
        one_minus_a2 = -jnp.tanh(log_a) * (a * a + 1.0)
        bbuf[:, sl] = jnp.sqrt(one_minus_a2) * (ig * xc[:, sl])

    row = lax.broadcasted_iota(jnp.int32, (SUBLANES, tn), 0)

    def group(g, hprev):
        r0 = pl.multiple_of(g * SUBLANES, SUBLANES)
        a = abuf[pl.ds(r0, SUBLANES), :]
        b = bbuf[pl.ds(r0, SUBLANES), :]
        for d in (1, 2, 4):
            keep = row >= d
            a_s = jnp.where(keep, pltpu.roll(a, d, 0), 1.0)
            b_s = jnp.where(keep, pltpu.roll(b, d, 0), 0.0)
            b = a * b_s + b
            a = a * a_s
        hg = a * hprev + b
        bbuf[pl.ds(r0, SUBLANES), :] = hg
        return hg[SUBLANES - 1:SUBLANES, :]

    hlast = lax.fori_loop(0, tm // SUBLANES, group, hcar[0:1, :])
    hcar[0:1, :] = hlast
    o_ref[...] = (bbuf[...] * gbuf[...]).astype(o_ref.dtype)


def _rnn_branch(h, w_in, b_in2, conv_w, conv_b, wa, ba, wi, bi, lam, seq, d_rnn):
    n, d = h.shape
    tm = _pick(TM_PROJ, seq)
    tn = _pick(TN_PROJ, d_rnn)
    assert tn % LRU_BLOCK == 0 and tm % SUBLANES == 0
    tpb = seq // tm
    kw = conv_w.shape[0]
    assert kw - 1 <= SUBLANES
    nj = d_rnn // tn
    hpt = tn // LRU_BLOCK
    vec = lambda off: pl.BlockSpec((1, tn), lambda j, i: (0, off + j))
    return pl.pallas_call(
        functools.partial(_rnn_kernel, tm=tm, tn=tn, tpb=tpb, kw=kw),
        out_shape=jax.ShapeDtypeStruct((n, d_rnn), BF16),
        grid=(nj, n // tm),
        in_specs=[pl.BlockSpec((tm, d), lambda j, i: (i, 0)),
                  pl.BlockSpec((d, tn), lambda j, i: (0, j)),
                  pl.BlockSpec((d, tn), lambda j, i: (0, nj + j)),
                  vec(0), vec(nj),
                  pl.BlockSpec((kw, tn), lambda j, i: (0, j)),
                  vec(0),
                  pl.BlockSpec((hpt, LRU_BLOCK, LRU_BLOCK), lambda j, i: (j, 0, 0)),
                  vec(0),
                  pl.BlockSpec((hpt, LRU_BLOCK, LRU_BLOCK), lambda j, i: (j, 0, 0)),
                  vec(0), vec(0)],
        out_specs=pl.BlockSpec((tm, tn), lambda j, i: (i, j)),
        scratch_shapes=[pltpu.VMEM((tm + SUBLANES, tn), F32),
                        pltpu.VMEM((tm, tn), F32),
                        pltpu.VMEM((tm, tn), F32),
                        pltpu.VMEM((tm, tn), F32),
                        pltpu.VMEM((SUBLANES, tn), F32)],
        compiler_params=_params(("arbitrary", "arbitrary")),
        name="rnn_branch",
    )(h, w_in, w_in, b_in2, b_in2, conv_w, conv_b.reshape(1, -1), wa, ba.reshape(1, -1),
      wi, bi.reshape(1, -1), lam.reshape(1, -1))


CONV_ROWS = 32


def _conv_kernel(h_ref, wv_ref, wg_ref, bv_ref, bg_ref, cw_ref, cb_ref, o_ref, ubuf,
                 *, tm, tn, tpb, kw, halo):
    i = pl.program_id(1)

    @pl.when(i % tpb == 0)
    def _():
        ubuf[0:halo, :] = jnp.zeros((halo, tn), F32)

    h = h_ref[...]
    val = jnp.dot(h, wv_ref[...], preferred_element_type=F32) + bv_ref[...]
    gate = jnp.dot(h, wg_ref[...], preferred_element_type=F32) + bg_ref[...]
    ubuf[halo:halo + tm, :] = val * jax.nn.sigmoid(gate)

    for c in range(tn // LANES):
        cs = slice(c * LANES, (c + 1) * LANES)

        def chunk(qi, carry):
            r0 = pl.multiple_of(qi * CONV_ROWS, CONV_ROWS)
            win = ubuf[pl.ds(r0, CONV_ROWS + halo), cs]
            acc = jnp.broadcast_to(cb_ref[:, cs], (CONV_ROWS, LANES))
            for r in range(SUBLANES):
                v = win if r == 0 else pltpu.roll(win, r, 0)
                for q in range(halo // SUBLANES):
                    back = SUBLANES * q + r
                    if back <= kw - 1:
                        lo = halo - SUBLANES * q
                        acc = acc + cw_ref[kw - 1 - back:kw - back, cs] * v[lo:lo + CONV_ROWS, :]
            o_ref[pl.ds(r0, CONV_ROWS), cs] = acc
            return carry

        lax.fori_loop(0, tm // CONV_ROWS, chunk, 0)

    ubuf[0:halo, :] = ubuf[tm:tm + halo, :]


def _conv_branch(h, w_in, b_in2, conv_w, conv_b, seq, d_conv, col0):
    n, d = h.shape
    tm = _pick(TM_PROJ, seq)
    tn = _pick(TN_PROJ, d_conv)
    tpb = seq // tm
    kw = conv_w.shape[0]
    halo = -(-(kw - 1) // SUBLANES) * SUBLANES
    assert tm % CONV_ROWS == 0 and tn % LANES == 0 and halo <= tm
    nj = d_conv // tn
    vec = lambda off: pl.BlockSpec((1, tn), lambda j, i: (0, off + j))
    return pl.pallas_call(
        functools.partial(_conv_kernel, tm=tm, tn=tn, tpb=tpb, kw=kw, halo=halo),
        out_shape=jax.ShapeDtypeStruct((n, d_conv), F32),
        grid=(nj, n // tm),
        in_specs=[pl.BlockSpec((tm, d), lambda j, i: (i, 0)),
                  pl.BlockSpec((d, tn), lambda j, i: (0, col0 * nj + j)),
                  pl.BlockSpec((d, tn), lambda j, i: (0, (col0 + 1) * nj + j)),
                  vec(col0 * nj), vec((col0 + 1) * nj),
                  pl.BlockSpec((kw, tn), lambda j, i: (0, j)),
                  vec(0)],
        out_specs=pl.BlockSpec((tm, tn), lambda j, i: (i, j)),
        scratch_shapes=[pltpu.VMEM((tm + halo, tn), F32)],
        compiler_params=_params(("arbitrary", "arbitrary")),
        name="conv_branch",
    )(h, w_in, w_in, b_in2, b_in2, conv_w, conv_b.reshape(1, -1))


def _gate_kernel(h_ref, wa_ref, wb_ref, ba_ref, bb_ref, oa_ref, ob_ref):
    h = h_ref[...]
    ga = jnp.dot(h, wa_ref[...], preferred_element_type=F32) + ba_ref[...]
    gb = jnp.dot(h, wb_ref[...], preferred_element_type=F32) + bb_ref[...]
    oa_ref[...] = jax.nn.sigmoid(ga).astype(oa_ref.dtype)
    ob_ref[...] = jax.nn.sigmoid(gb).astype(ob_ref.dtype)


def _gate_proj(h, w_in, b_in2, seq, d_model, col0):
    n, d = h.shape
    tm = _pick(TM_GATE, seq)
    tn = _pick(TN_PROJ, d_model)
    nj = d_model // tn
    vec = lambda off: pl.BlockSpec((1, tn), lambda j, i: (0, off + j))
    out = jax.ShapeDtypeStruct((n, d_model), BF16)
    return pl.pallas_call(
        _gate_kernel,
        out_shape=(out, out),
        grid=(nj, n // tm),
        in_specs=[pl.BlockSpec((tm, d), lambda j, i: (i, 0)),
                  pl.BlockSpec((d, tn), lambda j, i: (0, col0 * nj + j)),
                  pl.BlockSpec((d, tn), lambda j, i: (0, (col0 + 1) * nj + j)),
                  vec(col0 * nj), vec((col0 + 1) * nj)],
        out_specs=(pl.BlockSpec((tm, tn), lambda j, i: (i, j)),
                   pl.BlockSpec((tm, tn), lambda j, i: (i, j))),
        compiler_params=_params(("arbitrary", "arbitrary")),
        name="gate_proj",
    )(h, w_in, w_in, b_in2, b_in2)


LN_ROWS = 32


def _merge_kernel(rg_ref, uc_ref, sa_ref, sb_ref, wr_ref, wc_ref, lg_ref, lb_ref, o_ref, ub):
    @pl.when(pl.program_id(1) == 0)
    def _():
        def rows(q, carry):
            r0 = pl.multiple_of(q * LN_ROWS, LN_ROWS)
            u = uc_ref[pl.ds(r0, LN_ROWS), :]
            mu = jnp.mean(u, axis=-1, keepdims=True)
            xc = u - mu
            var = jnp.mean(xc * xc, axis=-1, keepdims=True)
            y = xc * lax.rsqrt(var + LN_EPS) * lg_ref[...] + lb_ref[...]
            ub[pl.ds(r0, LN_ROWS), :] = _silu(y).astype(ub.dtype)
            return carry

        lax.fori_loop(0, uc_ref.shape[0] // LN_ROWS, rows, 0)

    a = jnp.dot(rg_ref[...], wr_ref[...], preferred_element_type=F32)
    b = jnp.dot(ub[...], wc_ref[...], preferred_element_type=F32)
    o_ref[...] = (sa_ref[...].astype(F32) * a + sb_ref[...].astype(F32) * b).astype(o_ref.dtype)


def _merge(rnn_g, uc, sa, sb, w_rnn_out, w_conv_out, ln_g, ln_b):
    n, d_rnn = rnn_g.shape
    d_conv = uc.shape[1]
    d_model = w_rnn_out.shape[1]
    tm = _pick(TM_MERGE, n)
    tn = _pick(TN_MERGE, d_model)
    assert tm % LN_ROWS == 0
    return pl.pallas_call(
        _merge_kernel,
        out_shape=jax.ShapeDtypeStruct((n, d_model), BF16),
        grid=(n // tm, d_model // tn),
        in_specs=[pl.BlockSpec((tm, d_rnn), lambda i, j: (i, 0)),
                  pl.BlockSpec((tm, d_conv), lambda i, j: (i, 0)),
                  pl.BlockSpec((tm, tn), lambda i, j: (i, j)),
                  pl.BlockSpec((tm, tn), lambda i, j: (i, j)),
                  pl.BlockSpec((d_rnn, tn), lambda i, j: (0, j)),
                  pl.BlockSpec((d_conv, tn), lambda i, j: (0, j)),
                  pl.BlockSpec((1, d_conv), lambda i, j: (0, 0)),
                  pl.BlockSpec((1, d_conv), lambda i, j: (0, 0))],
        out_specs=pl.BlockSpec((tm, tn), lambda i, j: (i, j)),
        scratch_shapes=[pltpu.VMEM((tm, d_conv), BF16)],
        compiler_params=_params(("arbitrary", "arbitrary")),
        name="merge",
    )(rnn_g, uc, sa, sb, w_rnn_out, w_conv_out, ln_g.reshape(1, -1), ln_b.reshape(1, -1))


def _out_kernel(m_ref, w_ref, x_ref, mod_ref, o_ref, *, gate_idx):
    mix = jnp.dot(m_ref[...], w_ref[...], preferred_element_type=F32)
    o_ref[...] = x_ref[...] + mod_ref[0, gate_idx:gate_idx + 1, :] * mix


def _out_proj(merged, w_o, x2d, mod3, seq, gate_idx):
    n, d = merged.shape
    d_model = w_o.shape[1]
    tm = _pick(TM_OUT, seq)
    tn = _pick(TN_OUT, d_model)
    tpb = seq // tm
    n_mod = mod3.shape[1]
    return pl.pallas_call(
        functools.partial(_out_kernel, gate_idx=gate_idx),
        out_shape=jax.ShapeDtypeStruct((n, d_model), F32),
        grid=(n // tm, d_model // tn),
        in_specs=[pl.BlockSpec((tm, d), lambda i, j: (i, 0)),
                  pl.BlockSpec((d, tn), lambda i, j: (0, j)),
                  pl.BlockSpec((tm, tn), lambda i, j: (i, j)),
                  pl.BlockSpec((1, n_mod, tn), lambda i, j: (i // tpb, 0, j))],
        out_specs=pl.BlockSpec((tm, tn), lambda i, j: (i, j)),
        compiler_params=_params(("arbitrary", "arbitrary")),
        name="out_proj",
    )(merged, w_o, x2d, mod3)


ROUTE_E0, ROUTE_E1, ROUTE_W0, ROUTE_W1, ROUTE_R0, ROUTE_R1 = range(6)


def _norm_route_kernel(x_ref, g_ref, mod_ref, rw_ref, rb_ref, h_ref, route_ref, cnt_ref, run,
                       *, shift_idx, scale_idx, n_groups, epg):
    @pl.when(pl.program_id(0) == 0)
    def _():
        run[...] = jnp.zeros_like(run)

    x = x_ref[...]
    ms = jnp.mean(x * x, axis=-1, keepdims=True)
    y = x * lax.rsqrt(ms + RMS_EPS) * g_ref[...]
    h = y * (1.0 + mod_ref[0, scale_idx:scale_idx + 1, :]) + mod_ref[0, shift_idx:shift_idx + 1, :]
    h_ref[...] = h

    n_exp = n_groups * epg
    logits = jnp.dot(h, rw_ref[...], preferred_element_type=F32,
                     precision=lax.Precision.HIGHEST) + rb_ref[...]
    tm = logits.shape[0]
    lane = lax.broadcasted_iota(jnp.int32, logits.shape, 1)
    lane_f = lane.astype(F32)
    big = float(4 * LANES)

    def first_argmax(v, vmax):
        return jnp.min(jnp.where(v == vmax, lane_f, big), axis=-1, keepdims=True)

    is_g = (lane >= n_exp) & (lane < n_exp + n_groups)
    gl = jnp.where(is_g, logits, NEG_BIG)
    gmax = jnp.max(gl, axis=-1, keepdims=True)
    gsum = jnp.sum(jnp.where(is_g, jnp.exp(gl - gmax), 0.0), axis=-1, keepdims=True)
    g_p = 1.0 / gsum
    g_idx = first_argmax(gl, gmax) - float(n_exp)

    lane_grp = jnp.zeros_like(lane_f)
    for g in range(1, n_groups):
        lane_grp = lane_grp + jnp.where(lane >= g * epg, 1.0, 0.0)
    in_grp = (lane < n_exp) & (lane_grp == g_idx)
    el = jnp.where(in_grp, logits, NEG_BIG)
    m1 = jnp.max(el, axis=-1, keepdims=True)
    i1 = first_argmax(el, m1)
    el2 = jnp.where(lane_f == i1, NEG_BIG, el)
    m2 = jnp.max(el2, axis=-1, keepdims=True)
    i2 = first_argmax(el2, m2)
    t = jnp.exp(m2 - m1)
    p1 = 1.0 / (1.0 + t)
    p2 = t * p1

    hit1 = lane_f == i1
    hit2 = lane_f == i2
    onehot = jnp.where(hit1 | hit2, 1.0, 0.0)
    r_id = lax.broadcasted_iota(jnp.int32, (tm, tm), 0)
    c_id = lax.broadcasted_iota(jnp.int32, (tm, tm), 1)
    earlier = jnp.where(c_id < r_id, 1.0, 0.0).astype(BF16)
    before = jnp.dot(earlier, onehot.astype(BF16), preferred_element_type=F32) + run[0:1, :]
    rank1 = jnp.sum(jnp.where(hit1, before, 0.0), axis=-1, keepdims=True)
    rank2 = jnp.sum(jnp.where(hit2, before, 0.0), axis=-1, keepdims=True)
    total = run[0:1, :] + jnp.sum(onehot, axis=0, keepdims=True)
    run[0:1, :] = total
    cnt_ref[...] = jnp.broadcast_to(total, cnt_ref.shape)

    route = jnp.zeros_like(logits)
    for idx, val in ((ROUTE_E0, i1), (ROUTE_E1, i2), (ROUTE_W0, g_p * p1), (ROUTE_W1, g_p * p2),
                     (ROUTE_R0, rank1), (ROUTE_R1, rank2)):
        route = jnp.where(lane == idx, val, route)
    route_ref[...] = route


def _norm_route(x1, g, mod3, rw, rb, seq, shift_idx, scale_idx, n_groups, epg):
    n, d = x1.shape
    tm = _pick(TM_NORM, seq)
    tpb = seq // tm
    n_mod = mod3.shape[1]
    return pl.pallas_call(
        functools.partial(_norm_route_kernel, shift_idx=shift_idx, scale_idx=scale_idx,
                          n_groups=n_groups, epg=epg),
        out_shape=(jax.ShapeDtypeStruct((n, d), F32), jax.ShapeDtypeStruct((n, LANES), F32),
                   jax.ShapeDtypeStruct((SUBLANES, LANES), F32)),
        grid=(n // tm,),
        in_specs=[pl.BlockSpec((tm, d), lambda i: (i, 0)),
                  pl.BlockSpec((1, d), lambda i: (0, 0)),
                  pl.BlockSpec((1, n_mod, d), lambda i: (i // tpb, 0, 0)),
                  pl.BlockSpec((d, LANES), lambda i: (0, 0)),
                  pl.BlockSpec((1, LANES), lambda i: (0, 0))],
        out_specs=(pl.BlockSpec((tm, d), lambda i: (i, 0)),
                   pl.BlockSpec((tm, LANES), lambda i: (i, 0)),
                   pl.BlockSpec((SUBLANES, LANES), lambda i: (0, 0))),
        scratch_shapes=[pltpu.VMEM((SUBLANES, LANES), F32)],
        compiler_params=_params(("arbitrary",)),
        name="norm_route",
    )(x1, g.reshape(1, d), mod3, rw, rb)


def _expert_up_kernel(be_ref, nused_ref, src_ref, nval_ref, tok_ref, h_hbm, w1_ref, w3_ref, act_ref,
                      xbuf, sem, *, bm):
    b = pl.program_id(0)
    n_used = nused_ref[0]

    def gather(blk, slot):
        first = src_ref[blk]
        last = first + nval_ref[blk] - 1

        def issue(q, carry):
            for u in range(DMA_UNROLL):
                r = q * DMA_UNROLL + u
                t = tok_ref[jnp.minimum(first + r, last)]
                pltpu.make_async_copy(h_hbm.at[pl.ds(t, 1), :], xbuf.at[slot, pl.ds(r, 1), :],
                                      sem.at[slot]).start()
            return carry

        lax.fori_loop(0, bm // DMA_UNROLL, issue, 0)

    @pl.when(b == 0)
    def _():
        gather(0, 0)

    @pl.when(b + 1 < n_used)
    def _():
        gather(b + 1, (b + 1) % 2)

    @pl.when(b < n_used)
    def _():
        slot = b % 2
        pltpu.make_async_copy(h_hbm.at[pl.ds(0, bm), :], xbuf.at[slot], sem.at[slot]).wait()
        x = xbuf[slot].astype(BF16)
        gate = jnp.dot(x, w1_ref[0], preferred_element_type=F32)
        up = jnp.dot(x, w3_ref[0], preferred_element_type=F32)
        act_ref[...] = (_silu(gate) * up).astype(act_ref.dtype)

    @pl.when(b >= n_used)
    def _():
        act_ref[...] = jnp.zeros_like(act_ref)


def _expert_up(plan, h2, w1, w3, bm):
    n_blocks = plan["block_e"].shape[0]
    assert bm % DMA_UNROLL == 0
    d = h2.shape[1]
    f = w1.shape[2]
    wspec = pl.BlockSpec((1, d, f), lambda b, be, *_: (be[b], 0, 0))
    grid_spec = pltpu.PrefetchScalarGridSpec(
        num_scalar_prefetch=5,
        grid=(n_blocks,),
        in_specs=[pl.BlockSpec(memory_space=pl.ANY), wspec, wspec],
        out_specs=pl.BlockSpec((bm, f), lambda b, *_: (b, 0)),
        scratch_shapes=[pltpu.VMEM((2, bm, d), F32), pltpu.SemaphoreType.DMA((2,))],
    )
    return pl.pallas_call(
        functools.partial(_expert_up_kernel, bm=bm),
        out_shape=jax.ShapeDtypeStruct((n_blocks * bm, f), BF16),
        grid_spec=grid_spec,
        compiler_params=_params(("arbitrary",)),
        name="expert_up",
    )(plan["block_e"], plan["n_used"], plan["src_start"], plan["n_valid"], plan["tok_sorted"],
      h2, w1, w3)


def _expert_down_kernel(be_ref, nused_ref, act_ref, w2_ref, y_ref):
    b = pl.program_id(0)

    @pl.when(b < nused_ref[0])
    def _():
        y_ref[...] = jnp.dot(act_ref[...], w2_ref[0], preferred_element_type=F32)

    @pl.when(b >= nused_ref[0])
    def _():
        y_ref[...] = jnp.zeros_like(y_ref)


def _expert_down(plan, act, w2, bm):
    n_blocks = plan["block_e"].shape[0]
    f, d = w2.shape[1], w2.shape[2]
    grid_spec = pltpu.PrefetchScalarGridSpec(
        num_scalar_prefetch=2,
        grid=(n_blocks,),
        in_specs=[pl.BlockSpec((bm, f), lambda b, be, nu: (b, 0)),
                  pl.BlockSpec((1, f, d), lambda b, be, nu: (be[b], 0, 0))],
        out_specs=pl.BlockSpec((bm, d), lambda b, be, nu: (b, 0)),
    )
    return pl.pallas_call(
        _expert_down_kernel,
        out_shape=jax.ShapeDtypeStruct((n_blocks * bm, d), F32),
        grid_spec=grid_spec,
        compiler_params=_params(("arbitrary",)),
        name="expert_down",
    )(plan["block_e"], plan["n_used"], act, w2)


def _combine_kernel(d0_ref, d1_ref, y_hbm, x_ref, route_ref, mod_ref, g_ref, o_ref,
                    ybuf, sem, *, tm, gate_idx):
    i = pl.program_id(0)
    n_tiles = pl.num_programs(0)

    def gather(tile, slot):
        def issue(q, carry):
            for u in range(DMA_UNROLL):
                r = q * DMA_UNROLL + u
                t = tile * tm + r
                for k, d_ref in enumerate((d0_ref, d1_ref)):
                    pltpu.make_async_copy(y_hbm.at[pl.ds(d_ref[t], 1), :],
                                          ybuf.at[slot, k, pl.ds(r, 1), :], sem.at[slot, k]).start()
            return carry

        lax.fori_loop(0, tm // DMA_UNROLL, issue, 0)

    @pl.when(i == 0)
    def _():
        gather(0, 0)

    @pl.when(i + 1 < n_tiles)
    def _():
        gather(i + 1, (i + 1) % 2)

    slot = i % 2
    for k in range(TOP_K):
        pltpu.make_async_copy(y_hbm.at[pl.ds(0, tm), :], ybuf.at[slot, k], sem.at[slot, k]).wait()
    route = route_ref[...]
    ffn = (route[:, ROUTE_W0:ROUTE_W0 + 1] * ybuf[slot, 0]
           + route[:, ROUTE_W1:ROUTE_W1 + 1] * ybuf[slot, 1])
    x = x_ref[...] + mod_ref[0, gate_idx:gate_idx + 1, :] * ffn
    ms = jnp.mean(x * x, axis=-1, keepdims=True)
    o_ref[...] = x * lax.rsqrt(ms + RMS_EPS) * g_ref[...]


def _combine(d0, d1, yb, x1, route, mod3, g, seq, gate_idx):
    n, d = x1.shape
    tm = _pick(TM_COMBINE, seq)
    assert tm % DMA_UNROLL == 0
    tpb = seq // tm
    n_mod = mod3.shape[1]
    grid_spec = pltpu.PrefetchScalarGridSpec(
        num_scalar_prefetch=2,
        grid=(n // tm,),
        in_specs=[pl.BlockSpec(memory_space=pl.ANY),
                  pl.BlockSpec((tm, d), lambda i, a, b: (i, 0)),
                  pl.BlockSpec((tm, LANES), lambda i, a, b: (i, 0)),
                  pl.BlockSpec((1, n_mod, d), lambda i, a, b: (i // tpb, 0, 0)),
                  pl.BlockSpec((1, d), lambda i, a, b: (0, 0))],
        out_specs=pl.BlockSpec((tm, d), lambda i, a, b: (i, 0)),
        scratch_shapes=[pltpu.VMEM((2, TOP_K, tm, d), F32), pltpu.SemaphoreType.DMA((2, TOP_K))],
    )
    return pl.pallas_call(
        functools.partial(_combine_kernel, tm=tm, gate_idx=gate_idx),
        out_shape=jax.ShapeDtypeStruct((n, d), F32),
        grid_spec=grid_spec,
        compiler_params=_params(("arbitrary",)),
        name="combine",
    )(d0, d1, yb, x1, route, mod3, g.reshape(1, d))


def _dispatch_plan(route, counts, n_exp, bm):
    n_tok = route.shape[0]
    n_assign = n_tok * TOP_K
    i32 = jnp.int32
    e0 = route[:, ROUTE_E0].astype(i32)
    e1 = route[:, ROUTE_E1].astype(i32)
    counts = counts.astype(i32)
    starts = jnp.cumsum(counts) - counts
    padded = ((counts + bm - 1) // bm) * bm
    pad_ends = jnp.cumsum(padded)
    pad_starts = pad_ends - padded
    eids = jnp.arange(n_exp, dtype=i32)

    def table(tab, idx):
        return jnp.sum(jnp.where(idx[:, None] == eids[None, :], tab[None, :], 0), axis=1)

    d0 = table(pad_starts, e0) + route[:, ROUTE_R0].astype(i32)
    d1 = table(pad_starts, e1) + route[:, ROUTE_R1].astype(i32)
    flat_e = jnp.stack([e0, e1], axis=1).reshape(n_assign)
    key = flat_e * n_assign + jnp.arange(n_assign, dtype=i32)
    tok_sorted = (jnp.sort(key) % n_assign) // TOP_K
    n_blocks = -(-n_assign // bm) + n_exp
    n_used = pad_ends[-1] // bm
    blk = jnp.minimum(jnp.arange(n_blocks, dtype=i32), n_used - 1)
    block_e = jnp.sum((blk[:, None] * bm >= pad_ends[None, :]).astype(i32), axis=1)
    k = blk - table(pad_starts, block_e) // bm
    src_start = table(starts, block_e) + k * bm
    n_valid = jnp.clip(table(counts, block_e) - k * bm, 1, bm)
    plan = dict(block_e=block_e, n_used=n_used.reshape(1), src_start=src_start, n_valid=n_valid,
                tok_sorted=tok_sorted)
    return plan, d0, d1


def kernel(x, c, ada_w, ada_b, norm_mix_g, w_in, b_in, rnn_conv_w, rnn_conv_b, lru_wa, lru_ba, lru_wx, lru_bx, lru_lambda, w_rnn_out, conf_dw_w, conf_dw_b, conf_ln_g, conf_ln_b, w_conv_out, w_o, norm_ffn_g, router_group_w, router_group_b, router_expert_w, router_expert_b, expert_w1, expert_w3, expert_w2, final_norm_g):
    bsz, seq, d = x.shape
    depth = ada_w.shape[0]
    n_mod = ada_w.shape[2] // d
    d_rnn = lru_lambda.shape[1]
    d_conv = conf_dw_b.shape[1]
    n_groups = router_group_w.shape[2]
    n_exp = router_expert_w.shape[2]
    epg = n_exp // n_groups
    assert depth == 1 and d_rnn == d and d_conv == d and n_exp + n_groups <= LANES
    n_tok = bsz * seq
    x2d = x.reshape(n_tok, d)
    l = 0

    mod = _ada_mod(c, ada_w[l], ada_b[l])
    mod3 = mod.reshape(bsz, n_mod, d)
    w_in_b = _cast_bf16(w_in[l])
    b_in2 = b_in[l].reshape(1, -1)

    h = _norm_mod(x2d, norm_mix_g[l], mod3, seq, 0, 1)
    rnn_g = _rnn_branch(h, w_in_b, b_in2, rnn_conv_w[l], rnn_conv_b[l],
                        lru_wa[l].astype(BF16), lru_ba[l], lru_wx[l].astype(BF16), lru_bx[l],
                        lru_lambda[l], seq, d_rnn)
    uc = _conv_branch(h, w_in_b, b_in2, conf_dw_w[l], conf_dw_b[l], seq, d_conv, 2)
    sa, sb = _gate_proj(h, w_in_b, b_in2, seq, d, 4)
    merged = _merge(rnn_g, uc, sa, sb, w_rnn_out[l].astype(BF16), w_conv_out[l].astype(BF16),
                    conf_ln_g[l], conf_ln_b[l])
    x1 = _out_proj(merged, w_o[l].astype(BF16), x2d, mod3, seq, 2)

    rw = jnp.zeros((d, LANES), F32).at[:, :n_exp].set(router_expert_w[l])
    rw = rw.at[:, n_exp:n_exp + n_groups].set(router_group_w[l])
    rb = jnp.zeros((1, LANES), F32).at[0, :n_exp].set(router_expert_b[l])
    rb = rb.at[0, n_exp:n_exp + n_groups].set(router_group_b[l])
    h2, route, counts = _norm_route(x1, norm_ffn_g[l], mod3, rw, rb, seq, 3, 4, n_groups, epg)

    bm = BM_EXPERT
    plan, d0, d1 = _dispatch_plan(route, counts[0, :n_exp], n_exp, bm)
    act = _expert_up(plan, h2, expert_w1[l].astype(BF16), expert_w3[l].astype(BF16), bm)
    yb = _expert_down(plan, act, expert_w2[l].astype(BF16), bm)
    out = _combine(d0, d1, yb, x1, route, mod3, final_norm_g, seq, 5)
    return out.reshape(bsz, seq, d)
```

```python
import functools

import jax
import jax.numpy as jnp
from jax import lax
from jax.experimental import pallas as pl
from jax.experimental.pallas import tpu as pltpu

F32 = jnp.float32
BF16 = jnp.bfloat16

RMS_EPS = 1e-6
LN_EPS = 1e-5
LRU_C = 8.0
LRU_BLOCK = 256
TOP_K = 2
SUBLANES = 8
LANES = 128
MXU_COLS = 256
NEG_BIG = -1e30

TM_PROJ = 512
TN_PROJ = 512
TM_MERGE = 512
TN_MERGE = 512
TM_OUT = 1024
TN_OUT = 512
TM_NORM = 256
TN_ADA = 512
TM_CAST = 512
BM_EXPERT = 256
TM_COMBINE = 256
DMA_UNROLL = 8
COMBINE_GROUPS = 4
GATHER_AHEAD = 2
GATHER_BUFS = GATHER_AHEAD + 1

VMEM_LIMIT = 56 * 1024 * 1024


def _pick(pref, dim):
    t = min(pref, dim)
    while dim % t:
        t -= 1
    return t


def _params(semantics):
    return pltpu.CompilerParams(dimension_semantics=semantics, vmem_limit_bytes=VMEM_LIMIT)


def _silu(v):
    return v * jax.nn.sigmoid(v)


def _exact_zero(v):
    bits = pltpu.bitcast(v, jnp.uint32)
    bits = lax.shift_right_logical(lax.shift_right_logical(bits, jnp.uint32(16)), jnp.uint32(16))
    return pltpu.bitcast(bits, F32)


HIGH_HALF = 0xFFFF0000


def _pack_pairs(v):
    k = v.shape[1] // 2
    lo = pltpu.bitcast(v[:, :k].astype(BF16).astype(F32), jnp.uint32)
    hi = pltpu.bitcast(v[:, k:].astype(BF16).astype(F32), jnp.uint32)
    return (hi & jnp.uint32(HIGH_HALF)) | lax.shift_right_logical(lo, jnp.uint32(16))


def _unpack_pairs(p):
    lo = pltpu.bitcast(lax.shift_left(p, jnp.uint32(16)), F32)
    hi = pltpu.bitcast(p & jnp.uint32(HIGH_HALF), F32)
    return lo, hi


def _ada_kernel(c_ref, w_ref, b_ref, o_ref):
    c = c_ref[...]
    ca = _silu(c).astype(BF16)
    o_ref[...] = jnp.dot(ca, w_ref[...].astype(BF16), preferred_element_type=F32) + b_ref[...]


def _ada_mod(c, ada_w, ada_b):
    bsz, d = c.shape
    n = ada_w.shape[1]
    rows = -(-bsz // SUBLANES) * SUBLANES
    c_pad = jnp.zeros((rows, d), F32).at[:bsz].set(c)
    tn = _pick(TN_ADA, n)
    out = pl.pallas_call(
        _ada_kernel,
        out_shape=jax.ShapeDtypeStruct((rows, n), F32),
        grid=(n // tn,),
        in_specs=[pl.BlockSpec((rows, d), lambda j: (0, 0)),
                  pl.BlockSpec((d, tn), lambda j: (0, j)),
                  pl.BlockSpec((1, tn), lambda j: (0, j))],
        out_specs=pl.BlockSpec((rows, tn), lambda j: (0, j)),
        compiler_params=_params(("arbitrary",)),
        name="ada_mod",
    )(c_pad, ada_w, ada_b.reshape(1, n))
    return out[:bsz]


def _cast_kernel(w_ref, o_ref):
    o_ref[...] = w_ref[...].astype(o_ref.dtype)


def _cast_bf16(w):
    k, n = w.shape
    tk = _pick(TM_CAST, k)
    tn = _pick(4096, n)
    return pl.pallas_call(
        _cast_kernel,
        out_shape=jax.ShapeDtypeStruct((k, n), BF16),
        grid=(k // tk, n // tn),
        in_specs=[pl.BlockSpec((tk, tn), lambda i, j: (i, j))],
        out_specs=pl.BlockSpec((tk, tn), lambda i, j: (i, j)),
        compiler_params=_params(("arbitrary", "arbitrary")),
        name="cast_bf16",
    )(w)


def _norm_mod_kernel(x_ref, g_ref, mod_ref, o_ref, *, shift_idx, scale_idx):
    x = x_ref[...]
    ms = jnp.mean(x * x, axis=-1, keepdims=True)
    y = x * lax.rsqrt(ms + RMS_EPS) * g_ref[...]
    scale = mod_ref[0, scale_idx:scale_idx + 1, :]
    shift = mod_ref[0, shift_idx:shift_idx + 1, :]
    o_ref[...] = (y * (1.0 + scale) + shift).astype(o_ref.dtype)


def _norm_mod(x2d, g, mod3, seq, shift_idx, scale_idx):
    n, d = x2d.shape
    tm = _pick(TM_NORM, seq)
    tpb = seq // tm
    n_mod = mod3.shape[1]
    return pl.pallas_call(
        functools.partial(_norm_mod_kernel, shift_idx=shift_idx, scale_idx=scale_idx),
        out_shape=jax.ShapeDtypeStruct((n, d), BF16),
        grid=(n // tm,),
        in_specs=[pl.BlockSpec((tm, d), lambda i: (i, 0)),
                  pl.BlockSpec((1, d), lambda i: (0, 0)),
                  pl.BlockSpec((1, n_mod, d), lambda i: (i // tpb, 0, 0))],
        out_specs=pl.BlockSpec((tm, d), lambda i: (i, 0)),
        compiler_params=_params(("arbitrary",)),
        name="norm_mod",
    )(x2d, g.reshape(1, d), mod3)


RNN_ROWS = 16
RNN_LAG = 1


def _rnn_kernel(h_ref, wx_ref, wy_ref, wm_ref, bx_ref, by_ref, bm_ref, cw_ref, cb_ref, wa_ref, ba_ref,
                wi_ref, bi_ref, lam_ref, o_ref, sm_ref, xbuf, abuf, bbuf, gbuf, rbuf, hcar,
                *, tm, tn, tpb, kw):
    i = pl.program_id(1)

    @pl.when(i % tpb == 0)
    def _():
        xbuf[0:SUBLANES, :] = jnp.zeros((SUBLANES, tn), F32)
        hcar[...] = jnp.zeros_like(hcar)

    h = h_ref[...]
    xr = jnp.dot(h, wx_ref[...], preferred_element_type=F32) + bx_ref[...]
    y = jnp.dot(h, wy_ref[...], preferred_element_type=F32) + by_ref[...]
    gbuf[...] = jax.nn.gelu(y)

    xbuf[SUBLANES:SUBLANES + tm, :] = xr
    xc = cb_ref[...] + cw_ref[kw - 1:kw, :] * xr
    for k in range(kw - 1):
        off = SUBLANES - (kw - 1) + k
        xc = xc + cw_ref[k:k + 1, :] * xbuf[off:off + tm, :]
    xbuf[0:SUBLANES, :] = xbuf[tm:tm + SUBLANES, :]
    abuf[...] = xc

    xcb = xc.astype(BF16)
    for hh in range(tn // LRU_BLOCK):
        sl = slice(hh * LRU_BLOCK, (hh + 1) * LRU_BLOCK)
        xs = xcb[:, sl]
        rbuf[:, sl] = jnp.dot(xs, wa_ref[hh], preferred_element_type=F32) + ba_ref[:, sl]
        bbuf[:, sl] = jnp.dot(xs, wi_ref[hh], preferred_element_type=F32) + bi_ref[:, sl]

    gm = jnp.dot(h, wm_ref[...], preferred_element_type=F32) + bm_ref[...]
    sm_ref[...] = jax.nn.sigmoid(gm).astype(sm_ref.dtype)

    row = lax.broadcasted_iota(jnp.int32, (SUBLANES, LRU_BLOCK), 0)
    for hh in range(tn // LRU_BLOCK):
        sl = slice(hh * LRU_BLOCK, (hh + 1) * LRU_BLOCK)
        sp = jax.nn.softplus(-lam_ref[:, sl])
        hprev = hcar[0:1, sl]
        done = []
        for c, r0 in enumerate(range(0, tm, RNN_ROWS)):
            rs = slice(r0, r0 + RNN_ROWS)
            rl = rbuf[rs, sl]
            if c >= RNN_LAG:
                z = jnp.concatenate([done[c - RNN_LAG]] * (LRU_BLOCK // LANES), axis=1)
                rl = rl + jnp.concatenate([z] * (RNN_ROWS // SUBLANES), axis=0)
            r = jax.nn.sigmoid(rl)
            ig = jax.nn.sigmoid(bbuf[rs, sl])
            log_a = (-LRU_C) * r * sp
            a = jnp.exp(log_a)
            one_minus_a2 = -jnp.tanh(log_a) * (a * a + 1.0)
            b = jnp.sqrt(one_minus_a2) * (ig * abuf[rs, sl])
            hs = []
            for g0 in range(0, RNN_ROWS, SUBLANES):
                ag = a[g0:g0 + SUBLANES, :]
                bg = b[g0:g0 + SUBLANES, :]
                for d in (1, 2, 4):
                    keep = row >= d
                    a_s = jnp.where(keep, pltpu.roll(ag, d, 0), 1.0)
                    b_s = jnp.where(keep, pltpu.roll(bg, d, 0), 0.0)
                    bg = ag * b_s + bg
                    ag = ag * a_s
                hg = ag * hprev + bg
                hprev = hg[SUBLANES - 1:SUBLANES, :]
                hs.append(hg)
            o_ref[rs, sl] = (jnp.concatenate(hs, axis=0) * gbuf[rs, sl]).astype(o_ref.dtype)
            done.append(_exact_zero(hs[-1][:, 0:LANES]))
        hcar[0:1, sl] = hprev


def _rnn_branch(h, w_in, b_in2, conv_w, conv_b, wa, ba, wi, bi, lam, seq, d_rnn, col_m):
    n, d = h.shape
    tm = _pick(TM_PROJ, seq)
    tn = _pick(TN_PROJ, d_rnn)
    assert tn % LRU_BLOCK == 0 and tm % RNN_ROWS == 0 and RNN_ROWS % SUBLANES == 0
    tpb = seq // tm
    kw = conv_w.shape[0]
    assert kw - 1 <= SUBLANES
    nj = d_rnn // tn
    hpt = tn // LRU_BLOCK
    vec = lambda off: pl.BlockSpec((1, tn), lambda j, i: (0, off + j))
    return pl.pallas_call(
        functools.partial(_rnn_kernel, tm=tm, tn=tn, tpb=tpb, kw=kw),
        out_shape=(jax.ShapeDtypeStruct((n, d_rnn), BF16), jax.ShapeDtypeStruct((n, d_rnn), BF16)),
        grid=(nj, n // tm),
        in_specs=[pl.BlockSpec((tm, d), lambda j, i: (i, 0)),
                  pl.BlockSpec((d, tn), lambda j, i: (0, j)),
                  pl.BlockSpec((d, tn), lambda j, i: (0, nj + j)),
                  pl.BlockSpec((d, tn), lambda j, i: (0, col_m * nj + j)),
                  vec(0), vec(nj), vec(col_m * nj),
                  pl.BlockSpec((kw, tn), lambda j, i: (0, j)),
                  vec(0),
                  pl.BlockSpec((hpt, LRU_BLOCK, LRU_BLOCK), lambda j, i: (j, 0, 0)),
                  vec(0),
                  pl.BlockSpec((hpt, LRU_BLOCK, LRU_BLOCK), lambda j, i: (j, 0, 0)),
                  vec(0), vec(0)],
        out_specs=(pl.BlockSpec((tm, tn), lambda j, i: (i, j)),
                   pl.BlockSpec((tm, tn), lambda j, i: (i, j))),
        scratch_shapes=[pltpu.VMEM((tm + SUBLANES, tn), F32)] + [pltpu.VMEM((tm, tn), F32)] * 4
        + [pltpu.VMEM((SUBLANES, tn), F32)],
        compiler_params=_params(("arbitrary", "arbitrary")),
        name="rnn_branch",
    )(h, w_in, w_in, w_in, b_in2, b_in2, b_in2, conv_w, conv_b.reshape(1, -1), wa, ba.reshape(1, -1),
      wi, bi.reshape(1, -1), lam.reshape(1, -1))


CONV_ROWS = 16


def _conv_kernel(h_ref, wv_ref, wg_ref, wm_ref, bv_ref, bg_ref, bm_ref, cw_ref, cb_ref, o_ref, sm_ref,
                 ubuf, *, tm, tn, tpb, kw, halo):
    i = pl.program_id(1)

    @pl.when(i % tpb == 0)
    def _():
        ubuf[0:halo, :] = jnp.zeros((halo, tn), F32)

    h = h_ref[...]
    chain = None
    for p0 in range(0, tn, MXU_COLS):
        ps = slice(p0, p0 + MXU_COLS)
        val = jnp.dot(h, wv_ref[:, ps], preferred_element_type=F32) + bv_ref[:, ps]
        gate = jnp.dot(h, wg_ref[:, ps], preferred_element_type=F32) + bg_ref[:, ps]
        ubuf[halo:halo + tm, ps] = val * jax.nn.sigmoid(gate)
        if p0 + MXU_COLS == tn:
            gm = jnp.dot(h, wm_ref[...], preferred_element_type=F32) + bm_ref[...]
            sm_ref[...] = jax.nn.sigmoid(gm).astype(sm_ref.dtype)
        for c in range(p0 // LANES, (p0 + MXU_COLS) // LANES):
            cs = slice(c * LANES, (c + 1) * LANES)
            bias = jnp.broadcast_to(cb_ref[:, cs], (CONV_ROWS, LANES))
            for r0 in range(0, tm, CONV_ROWS):
                win = ubuf[r0:r0 + CONV_ROWS + halo, cs]
                if chain is not None:
                    win = win + jnp.concatenate([chain] * ((CONV_ROWS + halo) // SUBLANES), axis=0)
                acc = bias
                for r in range(SUBLANES):
                    v = win if r == 0 else pltpu.roll(win, r, 0)
                    for q in range(halo // SUBLANES):
                        back = SUBLANES * q + r
                        if back <= kw - 1:
                            lo = halo - SUBLANES * q
                            acc = acc + cw_ref[kw - 1 - back:kw - back, cs] * v[lo:lo + CONV_ROWS, :]
                o_ref[r0:r0 + CONV_ROWS, cs] = acc
                chain = _exact_zero(acc[0:SUBLANES, :])

    ubuf[0:halo, :] = ubuf[tm:tm + halo, :]


def _conv_branch(h, w_in, b_in2, conv_w, conv_b, seq, d_conv, col0, col_m):
    n, d = h.shape
    tm = _pick(TM_PROJ, seq)
    tn = _pick(TN_PROJ, d_conv)
    tpb = seq // tm
    kw = conv_w.shape[0]
    halo = -(-(kw - 1) // SUBLANES) * SUBLANES
    assert tm % CONV_ROWS == 0 and tn % MXU_COLS == 0 and halo <= tm
    nj = d_conv // tn
    vec = lambda off: pl.BlockSpec((1, tn), lambda j, i: (0, off + j))
    mat = lambda sec: pl.BlockSpec((d, tn), lambda j, i: (0, sec * nj + j))
    return pl.pallas_call(
        functools.partial(_conv_kernel, tm=tm, tn=tn, tpb=tpb, kw=kw, halo=halo),
        out_shape=(jax.ShapeDtypeStruct((n, d_conv), F32), jax.ShapeDtypeStruct((n, d_conv), BF16)),
        grid=(nj, n // tm),
        in_specs=[pl.BlockSpec((tm, d), lambda j, i: (i, 0)),
                  mat(col0), mat(col0 + 1), mat(col_m),
                  vec(col0 * nj), vec((col0 + 1) * nj), vec(col_m * nj),
                  pl.BlockSpec((kw, tn), lambda j, i: (0, j)),
                  vec(0)],
        out_specs=(pl.BlockSpec((tm, tn), lambda j, i: (i, j)),
                   pl.BlockSpec((tm, tn), lambda j, i: (i, j))),
        scratch_shapes=[pltpu.VMEM((tm + halo, tn), F32)],
        compiler_params=_params(("arbitrary", "arbitrary")),
        name="conv_branch",
    )(h, w_in, w_in, w_in, b_in2, b_in2, b_in2, conv_w, conv_b.reshape(1, -1))


LN_ROWS = 32
LN_UNROLL = 4


def _merge_kernel(rg_ref, uc_ref, sa_ref, sb_ref, wr_ref, wc_ref, lg_ref, lb_ref, o_ref, ub):
    @pl.when(pl.program_id(1) == 0)
    def _():
        def rows(q, carry):
            r0 = pl.multiple_of(q * LN_ROWS, LN_ROWS)
            u = uc_ref[pl.ds(r0, LN_ROWS), :]
            mu = jnp.mean(u, axis=-1, keepdims=True)
            xc = u - mu
            var = jnp.mean(xc * xc, axis=-1, keepdims=True)
            y = xc * lax.rsqrt(var + LN_EPS) * lg_ref[...] + lb_ref[...]
            ub[pl.ds(r0, LN_ROWS), :] = _silu(y).astype(ub.dtype)
            return carry

        lax.fori_loop(0, uc_ref.shape[0] // LN_ROWS, rows, 0, unroll=LN_UNROLL)

    a = jnp.dot(rg_ref[...], wr_ref[...], preferred_element_type=F32)
    b = jnp.dot(ub[...], wc_ref[...], preferred_element_type=F32)
    o_ref[...] = (sa_ref[...].astype(F32) * a + sb_ref[...].astype(F32) * b).astype(o_ref.dtype)


def _merge(rnn_g, uc, sa, sb, w_rnn_out, w_conv_out, ln_g, ln_b):
    n, d_rnn = rnn_g.shape
    d_conv = uc.shape[1]
    d_model = w_rnn_out.shape[1]
    tm = _pick(TM_MERGE, n)
    tn = _pick(TN_MERGE, d_model)
    assert tm % LN_ROWS == 0
    return pl.pallas_call(
        _merge_kernel,
        out_shape=jax.ShapeDtypeStruct((n, d_model), BF16),
        grid=(n // tm, d_model // tn),
        in_specs=[pl.BlockSpec((tm, d_rnn), lambda i, j: (i, 0)),
                  pl.BlockSpec((tm, d_conv), lambda i, j: (i, 0)),
                  pl.BlockSpec((tm, tn), lambda i, j: (i, j)),
                  pl.BlockSpec((tm, tn), lambda i, j: (i, j)),
                  pl.BlockSpec((d_rnn, tn), lambda i, j: (0, j)),
                  pl.BlockSpec((d_conv, tn), lambda i, j: (0, j)),
                  pl.BlockSpec((1, d_conv), lambda i, j: (0, 0)),
                  pl.BlockSpec((1, d_conv), lambda i, j: (0, 0))],
        out_specs=pl.BlockSpec((tm, tn), lambda i, j: (i, j)),
        scratch_shapes=[pltpu.VMEM((tm, d_conv), BF16)],
        compiler_params=_params(("arbitrary", "arbitrary")),
        name="merge",
    )(rnn_g, uc, sa, sb, w_rnn_out, w_conv_out, ln_g.reshape(1, -1), ln_b.reshape(1, -1))


def _out_kernel(m_ref, w_ref, x_ref, mod_ref, o_ref, *, gate_idx):
    mix = jnp.dot(m_ref[...], w_ref[...], preferred_element_type=F32)
    o_ref[...] = x_ref[...] + mod_ref[0, gate_idx:gate_idx + 1, :] * mix


def _out_proj(merged, w_o, x2d, mod3, seq, gate_idx):
    n, d = merged.shape
    d_model = w_o.shape[1]
    tm = _pick(TM_OUT, seq)
    tn = _pick(TN_OUT, d_model)
    tpb = seq // tm
    n_mod = mod3.shape[1]
    return pl.pallas_call(
        functools.partial(_out_kernel, gate_idx=gate_idx),
        out_shape=jax.ShapeDtypeStruct((n, d_model), F32),
        grid=(n // tm, d_model // tn),
        in_specs=[pl.BlockSpec((tm, d), lambda i, j: (i, 0)),
                  pl.BlockSpec((d, tn), lambda i, j: (0, j)),
                  pl.BlockSpec((tm, tn), lambda i, j: (i, j)),
                  pl.BlockSpec((1, n_mod, tn), lambda i, j: (i // tpb, 0, j))],
        out_specs=pl.BlockSpec((tm, tn), lambda i, j: (i, j)),
        compiler_params=_params(("arbitrary", "arbitrary")),
        name="out_proj",
    )(merged, w_o, x2d, mod3)


ROUTE_E0, ROUTE_E1, ROUTE_W0, ROUTE_W1, ROUTE_R0, ROUTE_R1 = range(6)


def _norm_route_kernel(x_ref, g_ref, mod_ref, rw_ref, rb_ref, h_ref, route_ref, cnt_ref, run,
                       *, shift_idx, scale_idx, n_groups, epg):
    @pl.when(pl.program_id(0) == 0)
    def _():
        run[...] = jnp.zeros_like(run)

    x = x_ref[...]
    ms = jnp.mean(x * x, axis=-1, keepdims=True)
    y = x * lax.rsqrt(ms + RMS_EPS) * g_ref[...]
    h = y * (1.0 + mod_ref[0, scale_idx:scale_idx + 1, :]) + mod_ref[0, shift_idx:shift_idx + 1, :]
    h_ref[...] = _pack_pairs(h)

    n_exp = n_groups * epg
    h_hi = h.astype(BF16)
    h_lo = (h - h_hi.astype(F32)).astype(BF16)
    w_hi = rw_ref[0]
    w_lo = rw_ref[1]
    logits = (jnp.dot(h_hi, w_hi, preferred_element_type=F32)
              + (jnp.dot(h_lo, w_hi, preferred_element_type=F32)
                 + jnp.dot(h_hi, w_lo, preferred_element_type=F32))) + rb_ref[...]
    tm = logits.shape[0]
    lane = lax.broadcasted_iota(jnp.int32, logits.shape, 1)
    lane_f = lane.astype(F32)
    big = float(4 * LANES)

    def first_argmax(v, vmax):
        return jnp.min(jnp.where(v == vmax, lane_f, big), axis=-1, keepdims=True)

    is_g = (lane >= n_exp) & (lane < n_exp + n_groups)
    gl = jnp.where(is_g, logits, NEG_BIG)
    gmax = jnp.max(gl, axis=-1, keepdims=True)
    gsum = jnp.sum(jnp.where(is_g, jnp.exp(gl - gmax), 0.0), axis=-1, keepdims=True)
    g_p = 1.0 / gsum
    g_idx = first_argmax(gl, gmax) - float(n_exp)

    lane_grp = jnp.zeros_like(lane_f)
    for g in range(1, n_groups):
        lane_grp = lane_grp + jnp.where(lane >= g * epg, 1.0, 0.0)
    in_grp = (lane < n_exp) & (lane_grp == g_idx)
    el = jnp.where(in_grp, logits, NEG_BIG)
    m1 = jnp.max(el, axis=-1, keepdims=True)
    i1 = first_argmax(el, m1)
    el2 = jnp.where(lane_f == i1, NEG_BIG, el)
    m2 = jnp.max(el2, axis=-1, keepdims=True)
    i2 = first_argmax(el2, m2)
    t = jnp.exp(m2 - m1)
    p1 = 1.0 / (1.0 + t)
    p2 = t * p1

    hit1 = lane_f == i1
    hit2 = lane_f == i2
    onehot = jnp.where(hit1 | hit2, 1.0, 0.0)
    r_id = lax.broadcasted_iota(jnp.int32, (tm, tm), 0)
    c_id = lax.broadcasted_iota(jnp.int32, (tm, tm), 1)
    earlier = jnp.where(c_id < r_id, 1.0, 0.0).astype(BF16)
    before = jnp.dot(earlier, onehot.astype(BF16), preferred_element_type=F32) + run[0:1, :]
    rank1 = jnp.sum(jnp.where(hit1, before, 0.0), axis=-1, keepdims=True)
    rank2 = jnp.sum(jnp.where(hit2, before, 0.0), axis=-1, keepdims=True)
    total = run[0:1, :] + jnp.sum(onehot, axis=0, keepdims=True)
    run[0:1, :] = total
    cnt_ref[...] = jnp.broadcast_to(total, cnt_ref.shape)

    route = jnp.zeros_like(logits)
    for idx, val in ((ROUTE_E0, i1), (ROUTE_E1, i2), (ROUTE_W0, g_p * p1), (ROUTE_W1, g_p * p2),
                     (ROUTE_R0, rank1), (ROUTE_R1, rank2)):
        route = jnp.where(lane == idx, val, route)
    route_ref[...] = route


def _norm_route(x1, g, mod3, rw, rb, seq, shift_idx, scale_idx, n_groups, epg):
    n, d = x1.shape
    tm = _pick(TM_NORM, seq)
    tpb = seq // tm
    n_mod = mod3.shape[1]
    return pl.pallas_call(
        functools.partial(_norm_route_kernel, shift_idx=shift_idx, scale_idx=scale_idx,
                          n_groups=n_groups, epg=epg),
        out_shape=(jax.ShapeDtypeStruct((n, d // 2), jnp.uint32), jax.ShapeDtypeStruct((n, LANES), F32),
                   jax.ShapeDtypeStruct((SUBLANES, LANES), F32)),
        grid=(n // tm,),
        in_specs=[pl.BlockSpec((tm, d), lambda i: (i, 0)),
                  pl.BlockSpec((1, d), lambda i: (0, 0)),
                  pl.BlockSpec((1, n_mod, d), lambda i: (i // tpb, 0, 0)),
                  pl.BlockSpec((2, d, LANES), lambda i: (0, 0, 0)),
                  pl.BlockSpec((1, LANES), lambda i: (0, 0))],
        out_specs=(pl.BlockSpec((tm, d // 2), lambda i: (i, 0)),
                   pl.BlockSpec((tm, LANES), lambda i: (i, 0)),
                   pl.BlockSpec((SUBLANES, LANES), lambda i: (0, 0))),
        scratch_shapes=[pltpu.VMEM((SUBLANES, LANES), F32)],
        compiler_params=_params(("arbitrary",)),
        name="norm_route",
    )(x1, g.reshape(1, d), mod3, rw, rb)


FF_PARTS = 2


def _first_of_expert(be_ref, b):
    return (b == 0) | (be_ref[b] != be_ref[jnp.maximum(b - 1, 0)])


def _expert_up_math(xp, w1_ref, w3_ref, w1b, w3b, act_ref, fresh, between=None):
    @pl.when(fresh)
    def _():
        w1b[...] = w1_ref[0].astype(BF16)
        w3b[...] = w3_ref[0].astype(BF16)

    lo, hi = _unpack_pairs(xp)
    lo = lo.astype(BF16)
    hi = hi.astype(BF16)
    k = lo.shape[1]
    operands = ((lo, w1b, 0), (hi, w1b, k), (lo, w3b, 0), (hi, w3b, k))
    parts = []
    for q, (x, w, k0) in enumerate(operands):
        if between is not None:
            between(q, len(operands))
        parts.append(jnp.dot(x, w[k0:k0 + k, :], preferred_element_type=F32))
    gate = parts[0] + parts[1]
    up = parts[2] + parts[3]
    act_ref[...] = (_silu(gate) * up).astype(act_ref.dtype)


def _expert_up_gather_kernel(be_ref, nused_ref, src_ref, nval_ref, tok_ref, h_hbm, w1_ref, w3_ref,
                             act_ref, xb_ref, xbuf, sem, w1b, w3b, *, bm):
    b = pl.program_id(0)
    n_used = nused_ref[0]

    def row_copy(first, last, slot, r):
        t = tok_ref[jnp.minimum(first + r, last)]
        return pltpu.make_async_copy(h_hbm.at[pl.ds(t, 1), :], xbuf.at[slot, pl.ds(r, 1), :],
                                     sem.at[slot])

    def wait_rows(slot):
        pltpu.make_async_copy(h_hbm.at[pl.ds(0, bm), :], xbuf.at[slot], sem.at[slot]).wait()

    def block_range(blk):
        first = src_ref[jnp.minimum(blk, n_used - 1)]
        return first, first + nval_ref[jnp.minimum(blk, n_used - 1)] - 1

    @pl.when(b == 0)
    def _():
        for blk in range(GATHER_AHEAD):
            first, last = block_range(blk)

            def issue(q, carry, first=first, last=last, blk=blk):
                for u in range(DMA_UNROLL):
                    row_copy(first, last, blk, q * DMA_UNROLL + u).start()
                return carry

            lax.fori_loop(0, bm // DMA_UNROLL, issue, 0)

    @pl.when(b < n_used)
    def _():
        slot = b % GATHER_BUFS
        wait_rows(slot)
        xp = xbuf[slot]
        xb_ref[...] = xp
        first, last = block_range(b + GATHER_AHEAD)
        ahead_slot = (b + GATHER_AHEAD) % GATHER_BUFS

        def start_rows(q, n):
            for r in range(q * bm // n, (q + 1) * bm // n):
                row_copy(first, last, ahead_slot, r).start()

        _expert_up_math(xp, w1_ref, w3_ref, w1b, w3b, act_ref, _first_of_expert(be_ref, b), start_rows)

    @pl.when(b + 1 == n_used)
    def _():
        for k in range(1, GATHER_AHEAD + 1):
            wait_rows((b + k) % GATHER_BUFS)

    @pl.when(b >= n_used)
    def _():
        act_ref[...] = jnp.zeros_like(act_ref)
        xb_ref[...] = jnp.zeros_like(xb_ref)


def _expert_up_kernel(be_ref, nused_ref, xb_ref, w1_ref, w3_ref, act_ref, w1b, w3b):
    b = pl.program_id(0)

    @pl.when(b < nused_ref[0])
    def _():
        _expert_up_math(xb_ref[...], w1_ref, w3_ref, w1b, w3b, act_ref, _first_of_expert(be_ref, b))

    @pl.when(b >= nused_ref[0])
    def _():
        act_ref[...] = jnp.zeros_like(act_ref)


def _expert_up(plan, hp, w1, w3, bm):
    n_blocks = plan["block_e"].shape[0]
    assert bm % DMA_UNROLL == 0
    half = hp.shape[1]
    d = 2 * half
    f = w1.shape[2]
    assert f % FF_PARTS == 0
    fp = f // FF_PARTS
    n_rows = n_blocks * bm
    wspec = lambda part: pl.BlockSpec((1, d, fp), lambda b, be, *_: (be[b], 0, part))
    act_spec = pl.BlockSpec((bm, fp), lambda b, *_: (b, 0))
    xb_spec = pl.BlockSpec((bm, half), lambda b, *_: (b, 0))
    w_scratch = [pltpu.VMEM((d, fp), BF16), pltpu.VMEM((d, fp), BF16)]
    act_shape = jax.ShapeDtypeStruct((n_rows, fp), BF16)

    act0, xb = pl.pallas_call(
        functools.partial(_expert_up_gather_kernel, bm=bm),
        out_shape=(act_shape, jax.ShapeDtypeStruct((n_rows, half), jnp.uint32)),
        grid_spec=pltpu.PrefetchScalarGridSpec(
            num_scalar_prefetch=5,
            grid=(n_blocks,),
            in_specs=[pl.BlockSpec(memory_space=pl.ANY), wspec(0), wspec(0)],
            out_specs=(act_spec, xb_spec),
            scratch_shapes=[pltpu.VMEM((GATHER_BUFS, bm, half), jnp.uint32),
                            pltpu.SemaphoreType.DMA((GATHER_BUFS,))] + w_scratch),
        compiler_params=_params(("arbitrary",)),
        name="expert_up_gather",
    )(plan["block_e"], plan["n_used"], plan["src_start"], plan["n_valid"], plan["tok_sorted"],
      hp, w1, w3)

    acts = [act0]
    for part in range(1, FF_PARTS):
        acts.append(pl.pallas_call(
            _expert_up_kernel,
            out_shape=act_shape,
            grid_spec=pltpu.PrefetchScalarGridSpec(
                num_scalar_prefetch=2,
                grid=(n_blocks,),
                in_specs=[xb_spec, wspec(part), wspec(part)],
                out_specs=act_spec,
                scratch_shapes=w_scratch),
            compiler_params=_params(("arbitrary",)),
            name="expert_up",
        )(plan["block_e"], plan["n_used"], xb, w1, w3))
    return acts


def _expert_down_kernel(be_ref, nused_ref, *refs):
    *act_refs, w2_ref, y_ref, w2b = refs
    b = pl.program_id(0)

    @pl.when(b < nused_ref[0])
    def _():
        @pl.when(_first_of_expert(be_ref, b))
        def _():
            w2b[...] = w2_ref[0].astype(BF16)

        y = None
        k0 = 0
        for a_ref in act_refs:
            k1 = k0 + a_ref.shape[1]
            part = jnp.dot(a_ref[...], w2b[k0:k1, :], preferred_element_type=F32)
            y = part if y is None else y + part
            k0 = k1
        y_ref[...] = _pack_pairs(y)

    @pl.when(b >= nused_ref[0])
    def _():
        y_ref[...] = jnp.zeros_like(y_ref)


def _expert_down(plan, acts, w2, bm):
    n_blocks = plan["block_e"].shape[0]
    f, d = w2.shape[1], w2.shape[2]
    act_specs = [pl.BlockSpec((bm, a.shape[1]), lambda b, be, nu: (b, 0)) for a in acts]
    grid_spec = pltpu.PrefetchScalarGridSpec(
        num_scalar_prefetch=2,
        grid=(n_blocks,),
        in_specs=act_specs + [pl.BlockSpec((1, f, d), lambda b, be, nu: (be[b], 0, 0))],
        out_specs=pl.BlockSpec((bm, d // 2), lambda b, be, nu: (b, 0)),
        scratch_shapes=[pltpu.VMEM((f, d), BF16)],
    )
    return pl.pallas_call(
        _expert_down_kernel,
        out_shape=jax.ShapeDtypeStruct((n_blocks * bm, d // 2), jnp.uint32),
        grid_spec=grid_spec,
        compiler_params=_params(("arbitrary",)),
        name="expert_down",
    )(plan["block_e"], plan["n_used"], *acts, w2)


def _combine_kernel(d0_ref, d1_ref, y_hbm, x_ref, route_ref, mod_ref, g_ref, o_ref,
                    ybuf, sem, *, tm, gate_idx):
    i = pl.program_id(0)
    n_tiles = pl.num_programs(0)

    def start_rows(tile, slot, rows):
        for r in rows:
            t = tile * tm + r
            for k, d_ref in enumerate((d0_ref, d1_ref)):
                pltpu.make_async_copy(y_hbm.at[pl.ds(d_ref[t], 1), :],
                                      ybuf.at[slot, k, pl.ds(r, 1), :], sem.at[slot, k]).start()

    def wait_rows(slot):
        for k in range(TOP_K):
            pltpu.make_async_copy(y_hbm.at[pl.ds(0, tm), :], ybuf.at[slot, k], sem.at[slot, k]).wait()

    @pl.when(i == 0)
    def _():
        for tile in range(GATHER_AHEAD):
            def issue(q, carry, tile=tile):
                start_rows(jnp.minimum(tile, n_tiles - 1), tile,
                           [q * DMA_UNROLL + u for u in range(DMA_UNROLL)])
                return carry

            lax.fori_loop(0, tm // DMA_UNROLL, issue, 0)

    slot = i % GATHER_BUFS
    wait_rows(slot)
    ahead = jnp.minimum(i + GATHER_AHEAD, n_tiles - 1)
    ahead_slot = (i + GATHER_AHEAD) % GATHER_BUFS
    half = ybuf.shape[-1]
    grp = tm // COMBINE_GROUPS
    for q in range(COMBINE_GROUPS):
        rows = slice(q * grp, (q + 1) * grp)
        start_rows(ahead, ahead_slot, range(q * grp, (q + 1) * grp))
        route = route_ref[rows, :]
        w0 = route[:, ROUTE_W0:ROUTE_W0 + 1]
        w1 = route[:, ROUTE_W1:ROUTE_W1 + 1]
        halves = zip(_unpack_pairs(ybuf[slot, 0, rows, :]), _unpack_pairs(ybuf[slot, 1, rows, :]),
                     (slice(0, half), slice(half, 2 * half)))
        xs = [x_ref[rows, cs] + mod_ref[0, gate_idx:gate_idx + 1, cs] * (w0 * y0 + w1 * y1)
              for y0, y1, cs in halves]
        ms = sum(jnp.sum(x * x, axis=-1, keepdims=True) for x in xs) / (2 * half)
        inv = lax.rsqrt(ms + RMS_EPS)
        o_ref[rows, 0:half] = xs[0] * inv * g_ref[:, 0:half]
        o_ref[rows, half:] = xs[1] * inv * g_ref[:, half:]

    @pl.when(i + 1 == n_tiles)
    def _():
        for k in range(1, GATHER_AHEAD + 1):
            wait_rows((i + k) % GATHER_BUFS)


def _combine(d0, d1, yb, x1, route, mod3, g, seq, gate_idx):
    n, d = x1.shape
    tm = _pick(TM_COMBINE, seq)
    assert tm % DMA_UNROLL == 0
    tpb = seq // tm
    n_mod = mod3.shape[1]
    grid_spec = pltpu.PrefetchScalarGridSpec(
        num_scalar_prefetch=2,
        grid=(n // tm,),
        in_specs=[pl.BlockSpec(memory_space=pl.ANY),
                  pl.BlockSpec((tm, d), lambda i, a, b: (i, 0)),
                  pl.BlockSpec((tm, LANES), lambda i, a, b: (i, 0)),
                  pl.BlockSpec((1, n_mod, d), lambda i, a, b: (i // tpb, 0, 0)),
                  pl.BlockSpec((1, d), lambda i, a, b: (0, 0))],
        out_specs=pl.BlockSpec((tm, d), lambda i, a, b: (i, 0)),
        scratch_shapes=[pltpu.VMEM((GATHER_BUFS, TOP_K, tm, d // 2), jnp.uint32),
                        pltpu.SemaphoreType.DMA((GATHER_BUFS, TOP_K))],
    )
    return pl.pallas_call(
        functools.partial(_combine_kernel, tm=tm, gate_idx=gate_idx),
        out_shape=jax.ShapeDtypeStruct((n, d), F32),
        grid_spec=grid_spec,
        compiler_params=_params(("arbitrary",)),
        name="combine",
    )(d0, d1, yb, x1, route, mod3, g.reshape(1, d))


def _dispatch_plan(route, counts, n_exp, bm):
    n_tok = route.shape[0]
    n_assign = n_tok * TOP_K
    i32 = jnp.int32
    e0 = route[:, ROUTE_E0].astype(i32)
    e1 = route[:, ROUTE_E1].astype(i32)
    counts = counts.astype(i32)
    starts = jnp.cumsum(counts) - counts
    padded = ((counts + bm - 1) // bm) * bm
    pad_ends = jnp.cumsum(padded)
    pad_starts = pad_ends - padded
    eids = jnp.arange(n_exp, dtype=i32)

    def table(tab, idx):
        return jnp.sum(jnp.where(idx[:, None] == eids[None, :], tab[None, :], 0), axis=1)

    d0 = table(pad_starts, e0) + route[:, ROUTE_R0].astype(i32)
    d1 = table(pad_starts, e1) + route[:, ROUTE_R1].astype(i32)
    flat_e = jnp.stack([e0, e1], axis=1).reshape(n_assign)
    key = flat_e * n_assign + jnp.arange(n_assign, dtype=i32)
    tok_sorted = (jnp.sort(key) % n_assign) // TOP_K
    n_blocks = -(-n_assign // bm) + n_exp
    n_used = pad_ends[-1] // bm
    blk = jnp.minimum(jnp.arange(n_blocks, dtype=i32), n_used - 1)
    block_e = jnp.sum((blk[:, None] * bm >= pad_ends[None, :]).astype(i32), axis=1)
    k = blk - table(pad_starts, block_e) // bm
    src_start = table(starts, block_e) + k * bm
    n_valid = jnp.clip(table(counts, block_e) - k * bm, 1, bm)
    plan = dict(block_e=block_e, n_used=n_used.reshape(1), src_start=src_start, n_valid=n_valid,
                tok_sorted=tok_sorted)
    return plan, d0, d1


def kernel(x, c, ada_w, ada_b, norm_mix_g, w_in, b_in, rnn_conv_w, rnn_conv_b, lru_wa, lru_ba, lru_wx, lru_bx, lru_lambda, w_rnn_out, conf_dw_w, conf_dw_b, conf_ln_g, conf_ln_b, w_conv_out, w_o, norm_ffn_g, router_group_w, router_group_b, router_expert_w, router_expert_b, expert_w1, expert_w3, expert_w2, final_norm_g):
    bsz, seq, d = x.shape
    depth = ada_w.shape[0]
    n_mod = ada_w.shape[2] // d
    d_rnn = lru_lambda.shape[1]
    d_conv = conf_dw_b.shape[1]
    n_groups = router_group_w.shape[2]
    n_exp = router_expert_w.shape[2]
    epg = n_exp // n_groups
    assert depth == 1 and d_rnn == d and d_conv == d and n_exp + n_groups <= LANES
    n_tok = bsz * seq
    x2d = x.reshape(n_tok, d)
    l = 0

    mod = _ada_mod(c, ada_w[l], ada_b[l])
    mod3 = mod.reshape(bsz, n_mod, d)
    w_in_b = _cast_bf16(w_in[l])
    b_in2 = b_in[l].reshape(1, -1)

    h = _norm_mod(x2d, norm_mix_g[l], mod3, seq, 0, 1)
    rnn_g, sa = _rnn_branch(h, w_in_b, b_in2, rnn_conv_w[l], rnn_conv_b[l],
                        lru_wa[l].astype(BF16), lru_ba[l], lru_wx[l].astype(BF16), lru_bx[l],
                        lru_lambda[l], seq, d_rnn, 4)
    uc, sb = _conv_branch(h, w_in_b, b_in2, conf_dw_w[l], conf_dw_b[l], seq, d_conv, 2, 5)
    merged = _merge(rnn_g, uc, sa, sb, w_rnn_out[l].astype(BF16), w_conv_out[l].astype(BF16),
                    conf_ln_g[l], conf_ln_b[l])
    x1 = _out_proj(merged, w_o[l].astype(BF16), x2d, mod3, seq, 2)

    rw = jnp.zeros((d, LANES), F32).at[:, :n_exp].set(router_expert_w[l])
    rw = rw.at[:, n_exp:n_exp + n_groups].set(router_group_w[l])
    rw_hi = rw.astype(BF16)
    rw = jnp.stack([rw_hi, (rw - rw_hi.astype(F32)).astype(BF16)])
    rb = jnp.zeros((1, LANES), F32).at[0, :n_exp].set(router_expert_b[l])
    rb = rb.at[0, n_exp:n_exp + n_groups].set(router_group_b[l])
    h2, route, counts = _norm_route(x1, norm_ffn_g[l], mod3, rw, rb, seq, 3, 4, n_groups, epg)

    bm = BM_EXPERT
    plan, d0, d1 = _dispatch_plan(route, counts[0, :n_exp], n_exp, bm)
    acts = _expert_up(plan, h2, expert_w1[l], expert_w3[l], bm)
    yb = _expert_down(plan, acts, expert_w2[l], bm)
    out = _combine(d0, d1, yb, x1, route, mod3, final_norm_g, seq, 5)
    return out.reshape(bsz, seq, d)
```

```python
import functools

import jax
import jax.numpy as jnp
from jax import lax
from jax.experimental import pallas as pl
from jax.experimental.pallas import tpu as pltpu

F32 = jnp.float32
BF16 = jnp.bfloat16

RMS_EPS = 1e-6
LN_EPS = 1e-5
LRU_C = 8.0
LRU_BLOCK = 256
TOP_K = 2
SUBLANES = 8
LANES = 128
MXU_COLS = 256
NEG_BIG = -1e30

TM_PROJ = 512
TN_PROJ = 512
TM_MERGE = 512
TN_MERGE = 512
TM_OUT = 1024
TN_OUT = 512
TM_NORM = 256
TN_ADA = 512
TM_CAST = 512
BM_EXPERT = 256
TM_COMBINE = 256
DMA_UNROLL = 8
COMBINE_GROUPS = 4
GATHER_AHEAD = 2
GATHER_BUFS = GATHER_AHEAD + 1

VMEM_LIMIT = 56 * 1024 * 1024


def _pick(pref, dim):
    t = min(pref, dim)
    while dim % t:
        t -= 1
    return t


def _params(semantics):
    return pltpu.CompilerParams(dimension_semantics=semantics, vmem_limit_bytes=VMEM_LIMIT)


def _silu(v):
    return v * jax.nn.sigmoid(v)


def _exact_zero(v):
    bits = pltpu.bitcast(v, jnp.uint32)
    bits = lax.shift_right_logical(lax.shift_right_logical(bits, jnp.uint32(16)), jnp.uint32(16))
    return pltpu.bitcast(bits, F32)


HIGH_HALF = 0xFFFF0000


def _pack_pairs(v):
    k = v.shape[1] // 2
    lo = pltpu.bitcast(v[:, :k].astype(BF16).astype(F32), jnp.uint32)
    hi = pltpu.bitcast(v[:, k:].astype(BF16).astype(F32), jnp.uint32)
    return (hi & jnp.uint32(HIGH_HALF)) | lax.shift_right_logical(lo, jnp.uint32(16))


def _unpack_pairs(p):
    lo = pltpu.bitcast(lax.shift_left(p, jnp.uint32(16)), F32)
    hi = pltpu.bitcast(p & jnp.uint32(HIGH_HALF), F32)
    return lo, hi


def _ada_kernel(c_ref, w_ref, b_ref, o_ref):
    c = c_ref[...]
    ca = _silu(c).astype(BF16)
    o_ref[...] = jnp.dot(ca, w_ref[...].astype(BF16), preferred_element_type=F32) + b_ref[...]


def _ada_mod(c, ada_w, ada_b):
    bsz, d = c.shape
    n = ada_w.shape[1]
    rows = -(-bsz // SUBLANES) * SUBLANES
    c_pad = jnp.zeros((rows, d), F32).at[:bsz].set(c)
    tn = _pick(TN_ADA, n)
    out = pl.pallas_call(
        _ada_kernel,
        out_shape=jax.ShapeDtypeStruct((rows, n), F32),
        grid=(n // tn,),
        in_specs=[pl.BlockSpec((rows, d), lambda j: (0, 0)),
                  pl.BlockSpec((d, tn), lambda j: (0, j)),
                  pl.BlockSpec((1, tn), lambda j: (0, j))],
        out_specs=pl.BlockSpec((rows, tn), lambda j: (0, j)),
        compiler_params=_params(("arbitrary",)),
        name="ada_mod",
    )(c_pad, ada_w, ada_b.reshape(1, n))
    return out[:bsz]


def _cast_kernel(w_ref, o_ref):
    o_ref[...] = w_ref[...].astype(o_ref.dtype)


def _cast_bf16(w):
    k, n = w.shape
    tk = _pick(TM_CAST, k)
    tn = _pick(4096, n)
    return pl.pallas_call(
        _cast_kernel,
        out_shape=jax.ShapeDtypeStruct((k, n), BF16),
        grid=(k // tk, n // tn),
        in_specs=[pl.BlockSpec((tk, tn), lambda i, j: (i, j))],
        out_specs=pl.BlockSpec((tk, tn), lambda i, j: (i, j)),
        compiler_params=_params(("arbitrary", "arbitrary")),
        name="cast_bf16",
    )(w)


def _norm_mod_kernel(x_ref, g_ref, mod_ref, o_ref, *, shift_idx, scale_idx):
    x = x_ref[...]
    ms = jnp.mean(x * x, axis=-1, keepdims=True)
    y = x * lax.rsqrt(ms + RMS_EPS) * g_ref[...]
    scale = mod_ref[0, scale_idx:scale_idx + 1, :]
    shift = mod_ref[0, shift_idx:shift_idx + 1, :]
    o_ref[...] = (y * (1.0 + scale) + shift).astype(o_ref.dtype)


def _norm_mod(x2d, g, mod3, seq, shift_idx, scale_idx):
    n, d = x2d.shape
    tm = _pick(TM_NORM, seq)
    tpb = seq // tm
    n_mod = mod3.shape[1]
    return pl.pallas_call(
        functools.partial(_norm_mod_kernel, shift_idx=shift_idx, scale_idx=scale_idx),
        out_shape=jax.ShapeDtypeStruct((n, d), BF16),
        grid=(n // tm,),
        in_specs=[pl.BlockSpec((tm, d), lambda i: (i, 0)),
                  pl.BlockSpec((1, d), lambda i: (0, 0)),
                  pl.BlockSpec((1, n_mod, d), lambda i: (i // tpb, 0, 0))],
        out_specs=pl.BlockSpec((tm, d), lambda i: (i, 0)),
        compiler_params=_params(("arbitrary",)),
        name="norm_mod",
    )(x2d, g.reshape(1, d), mod3)


RNN_ROWS = 16
RNN_LAG = 1


def _rnn_kernel(h_ref, wx_ref, wy_ref, wm_ref, bx_ref, by_ref, bm_ref, cw_ref, cb_ref, wa_ref, ba_ref,
                wi_ref, bi_ref, lam_ref, o_ref, sm_ref, xbuf, abuf, bbuf, gbuf, rbuf, hcar,
                *, tm, tn, tpb, kw):
    i = pl.program_id(1)

    @pl.when(i % tpb == 0)
    def _():
        xbuf[0:SUBLANES, :] = jnp.zeros((SUBLANES, tn), F32)
        hcar[...] = jnp.zeros_like(hcar)

    h = h_ref[...]
    xr = jnp.dot(h, wx_ref[...], preferred_element_type=F32) + bx_ref[...]
    y = jnp.dot(h, wy_ref[...], preferred_element_type=F32) + by_ref[...]
    gbuf[...] = jax.nn.gelu(y)

    xbuf[SUBLANES:SUBLANES + tm, :] = xr
    xc = cb_ref[...] + cw_ref[kw - 1:kw, :] * xr
    for k in range(kw - 1):
        off = SUBLANES - (kw - 1) + k
        xc = xc + cw_ref[k:k + 1, :] * xbuf[off:off + tm, :]
    xbuf[0:SUBLANES, :] = xbuf[tm:tm + SUBLANES, :]
    abuf[...] = xc

    xcb = xc.astype(BF16)
    for hh in range(tn // LRU_BLOCK):
        sl = slice(hh * LRU_BLOCK, (hh + 1) * LRU_BLOCK)
        xs = xcb[:, sl]
        rbuf[:, sl] = jnp.dot(xs, wa_ref[hh], preferred_element_type=F32) + ba_ref[:, sl]
        bbuf[:, sl] = jnp.dot(xs, wi_ref[hh], preferred_element_type=F32) + bi_ref[:, sl]

    gm = jnp.dot(h, wm_ref[...], preferred_element_type=F32) + bm_ref[...]
    sm_ref[...] = jax.nn.sigmoid(gm).astype(sm_ref.dtype)

    row = lax.broadcasted_iota(jnp.int32, (SUBLANES, LRU_BLOCK), 0)
    for hh in range(tn // LRU_BLOCK):
        sl = slice(hh * LRU_BLOCK, (hh + 1) * LRU_BLOCK)
        sp = jax.nn.softplus(-lam_ref[:, sl])
        hprev = hcar[0:1, sl]
        done = []
        for c, r0 in enumerate(range(0, tm, RNN_ROWS)):
            rs = slice(r0, r0 + RNN_ROWS)
            rl = rbuf[rs, sl]
            if c >= RNN_LAG:
                z = jnp.concatenate([done[c - RNN_LAG]] * (LRU_BLOCK // LANES), axis=1)
                rl = rl + jnp.concatenate([z] * (RNN_ROWS // SUBLANES), axis=0)
            r = jax.nn.sigmoid(rl)
            ig = jax.nn.sigmoid(bbuf[rs, sl])
            log_a = (-LRU_C) * r * sp
            a = jnp.exp(log_a)
            one_minus_a2 = -jnp.tanh(log_a) * (a * a + 1.0)
            b = jnp.sqrt(one_minus_a2) * (ig * abuf[rs, sl])
            hs = []
            for g0 in range(0, RNN_ROWS, SUBLANES):
                ag = a[g0:g0 + SUBLANES, :]
                bg = b[g0:g0 + SUBLANES, :]
                for d in (1, 2, 4):
                    keep = row >= d
                    a_s = jnp.where(keep, pltpu.roll(ag, d, 0), 1.0)
                    b_s = jnp.where(keep, pltpu.roll(bg, d, 0), 0.0)
                    bg = ag * b_s + bg
                    ag = ag * a_s
                hg = ag * hprev + bg
                hprev = hg[SUBLANES - 1:SUBLANES, :]
                hs.append(hg)
            o_ref[rs, sl] = (jnp.concatenate(hs, axis=0) * gbuf[rs, sl]).astype(o_ref.dtype)
            done.append(_exact_zero(hs[-1][:, 0:LANES]))
        hcar[0:1, sl] = hprev


def _rnn_branch(h, w_in, b_in2, conv_w, conv_b, wa, ba, wi, bi, lam, seq, d_rnn, col_m):
    n, d = h.shape
    tm = _pick(TM_PROJ, seq)
    tn = _pick(TN_PROJ, d_rnn)
    assert tn % LRU_BLOCK == 0 and tm % RNN_ROWS == 0 and RNN_ROWS % SUBLANES == 0
    tpb = seq // tm
    kw = conv_w.shape[0]
    assert kw - 1 <= SUBLANES
    nj = d_rnn // tn
    hpt = tn // LRU_BLOCK
    vec = lambda off: pl.BlockSpec((1, tn), lambda j, i: (0, off + j))
    return pl.pallas_call(
        functools.partial(_rnn_kernel, tm=tm, tn=tn, tpb=tpb, kw=kw),
        out_shape=(jax.ShapeDtypeStruct((n, d_rnn), BF16), jax.ShapeDtypeStruct((n, d_rnn), BF16)),
        grid=(nj, n // tm),
        in_specs=[pl.BlockSpec((tm, d), lambda j, i: (i, 0)),
                  pl.BlockSpec((d, tn), lambda j, i: (0, j)),
                  pl.BlockSpec((d, tn), lambda j, i: (0, nj + j)),
                  pl.BlockSpec((d, tn), lambda j, i: (0, col_m * nj + j)),
                  vec(0), vec(nj), vec(col_m * nj),
                  pl.BlockSpec((kw, tn), lambda j, i: (0, j)),
                  vec(0),
                  pl.BlockSpec((hpt, LRU_BLOCK, LRU_BLOCK), lambda j, i: (j, 0, 0)),
                  vec(0),
                  pl.BlockSpec((hpt, LRU_BLOCK, LRU_BLOCK), lambda j, i: (j, 0, 0)),
                  vec(0), vec(0)],
        out_specs=(pl.BlockSpec((tm, tn), lambda j, i: (i, j)),
                   pl.BlockSpec((tm, tn), lambda j, i: (i, j))),
        scratch_shapes=[pltpu.VMEM((tm + SUBLANES, tn), F32)] + [pltpu.VMEM((tm, tn), F32)] * 4
        + [pltpu.VMEM((SUBLANES, tn), F32)],
        compiler_params=_params(("arbitrary", "arbitrary")),
        name="rnn_branch",
    )(h, w_in, w_in, w_in, b_in2, b_in2, b_in2, conv_w, conv_b.reshape(1, -1), wa, ba.reshape(1, -1),
      wi, bi.reshape(1, -1), lam.reshape(1, -1))


CONV_ROWS = 16


def _conv_kernel(h_ref, wv_ref, wg_ref, wm_ref, bv_ref, bg_ref, bm_ref, cw_ref, cb_ref, o_ref, sm_ref,
                 ubuf, *, tm, tn, tpb, kw, halo):
    i = pl.program_id(1)

    @pl.when(i % tpb == 0)
    def _():
        ubuf[0:halo, :] = jnp.zeros((halo, tn), F32)

    h = h_ref[...]
    chain = None
    for p0 in range(0, tn, MXU_COLS):
        ps = slice(p0, p0 + MXU_COLS)
        val = jnp.dot(h, wv_ref[:, ps], preferred_element_type=F32) + bv_ref[:, ps]
        gate = jnp.dot(h, wg_ref[:, ps], preferred_element_type=F32) + bg_ref[:, ps]
        ubuf[halo:halo + tm, ps] = val * jax.nn.sigmoid(gate)
        if p0 + MXU_COLS == tn:
            gm = jnp.dot(h, wm_ref[...], preferred_element_type=F32) + bm_ref[...]
            sm_ref[...] = jax.nn.sigmoid(gm).astype(sm_ref.dtype)
        for c in range(p0 // LANES, (p0 + MXU_COLS) // LANES):
            cs = slice(c * LANES, (c + 1) * LANES)
            bias = jnp.broadcast_to(cb_ref[:, cs], (CONV_ROWS, LANES))
            for r0 in range(0, tm, CONV_ROWS):
                win = ubuf[r0:r0 + CONV_ROWS + halo, cs]
                if chain is not None:
                    win = win + jnp.concatenate([chain] * ((CONV_ROWS + halo) // SUBLANES), axis=0)
                acc = bias
                for r in range(SUBLANES):
                    v = win if r == 0 else pltpu.roll(win, r, 0)
                    for q in range(halo // SUBLANES):
                        back = SUBLANES * q + r
                        if back <= kw - 1:
                            lo = halo - SUBLANES * q
                            acc = acc + cw_ref[kw - 1 - back:kw - back, cs] * v[lo:lo + CONV_ROWS, :]
                o_ref[r0:r0 + CONV_ROWS, cs] = acc
                chain = _exact_zero(acc[0:SUBLANES, :])

    ubuf[0:halo, :] = ubuf[tm:tm + halo, :]


def _conv_branch(h, w_in, b_in2, conv_w, conv_b, seq, d_conv, col0, col_m):
    n, d = h.shape
    tm = _pick(TM_PROJ, seq)
    tn = _pick(TN_PROJ, d_conv)
    tpb = seq // tm
    kw = conv_w.shape[0]
    halo = -(-(kw - 1) // SUBLANES) * SUBLANES
    assert tm % CONV_ROWS == 0 and tn % MXU_COLS == 0 and halo <= tm
    nj = d_conv // tn
    vec = lambda off: pl.BlockSpec((1, tn), lambda j, i: (0, off + j))
    mat = lambda sec: pl.BlockSpec((d, tn), lambda j, i: (0, sec * nj + j))
    return pl.pallas_call(
        functools.partial(_conv_kernel, tm=tm, tn=tn, tpb=tpb, kw=kw, halo=halo),
        out_shape=(jax.ShapeDtypeStruct((n, d_conv), F32), jax.ShapeDtypeStruct((n, d_conv), BF16)),
        grid=(nj, n // tm),
        in_specs=[pl.BlockSpec((tm, d), lambda j, i: (i, 0)),
                  mat(col0), mat(col0 + 1), mat(col_m),
                  vec(col0 * nj), vec((col0 + 1) * nj), vec(col_m * nj),
                  pl.BlockSpec((kw, tn), lambda j, i: (0, j)),
                  vec(0)],
        out_specs=(pl.BlockSpec((tm, tn), lambda j, i: (i, j)),
                   pl.BlockSpec((tm, tn), lambda j, i: (i, j))),
        scratch_shapes=[pltpu.VMEM((tm + halo, tn), F32)],
        compiler_params=_params(("arbitrary", "arbitrary")),
        name="conv_branch",
    )(h, w_in, w_in, w_in, b_in2, b_in2, b_in2, conv_w, conv_b.reshape(1, -1))


LN_ROWS = 32
LN_UNROLL = 4


def _merge_kernel(rg_ref, uc_ref, sa_ref, sb_ref, wr_ref, wc_ref, lg_ref, lb_ref, o_ref, ub):
    @pl.when(pl.program_id(1) == 0)
    def _():
        def rows(q, carry):
            r0 = pl.multiple_of(q * LN_ROWS, LN_ROWS)
            u = uc_ref[pl.ds(r0, LN_ROWS), :]
            mu = jnp.mean(u, axis=-1, keepdims=True)
            xc = u - mu
            var = jnp.mean(xc * xc, axis=-1, keepdims=True)
            y = xc * lax.rsqrt(var + LN_EPS) * lg_ref[...] + lb_ref[...]
            ub[pl.ds(r0, LN_ROWS), :] = _silu(y).astype(ub.dtype)
            return carry

        lax.fori_loop(0, uc_ref.shape[0] // LN_ROWS, rows, 0, unroll=LN_UNROLL)

    a = jnp.dot(rg_ref[...], wr_ref[...], preferred_element_type=F32)
    b = jnp.dot(ub[...], wc_ref[...], preferred_element_type=F32)
    o_ref[...] = (sa_ref[...].astype(F32) * a + sb_ref[...].astype(F32) * b).astype(o_ref.dtype)


def _merge(rnn_g, uc, sa, sb, w_rnn_out, w_conv_out, ln_g, ln_b):
    n, d_rnn = rnn_g.shape
    d_conv = uc.shape[1]
    d_model = w_rnn_out.shape[1]
    tm = _pick(TM_MERGE, n)
    tn = _pick(TN_MERGE, d_model)
    assert tm % LN_ROWS == 0
    return pl.pallas_call(
        _merge_kernel,
        out_shape=jax.ShapeDtypeStruct((n, d_model), BF16),
        grid=(n // tm, d_model // tn),
        in_specs=[pl.BlockSpec((tm, d_rnn), lambda i, j: (i, 0)),
                  pl.BlockSpec((tm, d_conv), lambda i, j: (i, 0)),
                  pl.BlockSpec((tm, tn), lambda i, j: (i, j)),
                  pl.BlockSpec((tm, tn), lambda i, j: (i, j)),
                  pl.BlockSpec((d_rnn, tn), lambda i, j: (0, j)),
                  pl.BlockSpec((d_conv, tn), lambda i, j: (0, j)),
                  pl.BlockSpec((1, d_conv), lambda i, j: (0, 0)),
                  pl.BlockSpec((1, d_conv), lambda i, j: (0, 0))],
        out_specs=pl.BlockSpec((tm, tn), lambda i, j: (i, j)),
        scratch_shapes=[pltpu.VMEM((tm, d_conv), BF16)],
        compiler_params=_params(("arbitrary", "arbitrary")),
        name="merge",
    )(rnn_g, uc, sa, sb, w_rnn_out, w_conv_out, ln_g.reshape(1, -1), ln_b.reshape(1, -1))


def _out_kernel(m_ref, w_ref, x_ref, mod_ref, o_ref, *, gate_idx):
    mix = jnp.dot(m_ref[...], w_ref[...], preferred_element_type=F32)
    o_ref[...] = x_ref[...] + mod_ref[0, gate_idx:gate_idx + 1, :] * mix


def _out_proj(merged, w_o, x2d, mod3, seq, gate_idx):
    n, d = merged.shape
    d_model = w_o.shape[1]
    tm = _pick(TM_OUT, seq)
    tn = _pick(TN_OUT, d_model)
    tpb = seq // tm
    n_mod = mod3.shape[1]
    return pl.pallas_call(
        functools.partial(_out_kernel, gate_idx=gate_idx),
        out_shape=jax.ShapeDtypeStruct((n, d_model), F32),
        grid=(n // tm, d_model // tn),
        in_specs=[pl.BlockSpec((tm, d), lambda i, j: (i, 0)),
                  pl.BlockSpec((d, tn), lambda i, j: (0, j)),
                  pl.BlockSpec((tm, tn), lambda i, j: (i, j)),
                  pl.BlockSpec((1, n_mod, tn), lambda i, j: (i // tpb, 0, j))],
        out_specs=pl.BlockSpec((tm, tn), lambda i, j: (i, j)),
        compiler_params=_params(("arbitrary", "arbitrary")),
        name="out_proj",
    )(merged, w_o, x2d, mod3)


ROUTE_E0, ROUTE_E1, ROUTE_W0, ROUTE_W1, ROUTE_R0, ROUTE_R1 = range(6)


def _norm_route_kernel(x_ref, g_ref, mod_ref, rw_ref, rb_ref, h_ref, route_ref, cnt_ref, run,
                       *, shift_idx, scale_idx, n_groups, epg):
    @pl.when(pl.program_id(0) == 0)
    def _():
        run[...] = jnp.zeros_like(run)

    x = x_ref[...]
    ms = jnp.mean(x * x, axis=-1, keepdims=True)
    y = x * lax.rsqrt(ms + RMS_EPS) * g_ref[...]
    h = y * (1.0 + mod_ref[0, scale_idx:scale_idx + 1, :]) + mod_ref[0, shift_idx:shift_idx + 1, :]
    h_ref[...] = _pack_pairs(h)

    n_exp = n_groups * epg
    h_hi = h.astype(BF16)
    h_lo = (h - h_hi.astype(F32)).astype(BF16)
    w_hi = rw_ref[0]
    w_lo = rw_ref[1]
    logits = (jnp.dot(h_hi, w_hi, preferred_element_type=F32)
              + (jnp.dot(h_lo, w_hi, preferred_element_type=F32)
                 + jnp.dot(h_hi, w_lo, preferred_element_type=F32))) + rb_ref[...]
    tm = logits.shape[0]
    lane = lax.broadcasted_iota(jnp.int32, logits.shape, 1)
    lane_f = lane.astype(F32)
    big = float(4 * LANES)

    def first_argmax(v, vmax):
        return jnp.min(jnp.where(v == vmax, lane_f, big), axis=-1, keepdims=True)

    is_g = (lane >= n_exp) & (lane < n_exp + n_groups)
    gl = jnp.where(is_g, logits, NEG_BIG)
    gmax = jnp.max(gl, axis=-1, keepdims=True)
    gsum = jnp.sum(jnp.where(is_g, jnp.exp(gl - gmax), 0.0), axis=-1, keepdims=True)
    g_p = 1.0 / gsum
    g_idx = first_argmax(gl, gmax) - float(n_exp)

    lane_grp = jnp.zeros_like(lane_f)
    for g in range(1, n_groups):
        lane_grp = lane_grp + jnp.where(lane >= g * epg, 1.0, 0.0)
    in_grp = (lane < n_exp) & (lane_grp == g_idx)
    el = jnp.where(in_grp, logits, NEG_BIG)
    m1 = jnp.max(el, axis=-1, keepdims=True)
    i1 = first_argmax(el, m1)
    el2 = jnp.where(lane_f == i1, NEG_BIG, el)
    m2 = jnp.max(el2, axis=-1, keepdims=True)
    i2 = first_argmax(el2, m2)
    t = jnp.exp(m2 - m1)
    p1 = 1.0 / (1.0 + t)
    p2 = t * p1

    hit1 = lane_f == i1
    hit2 = lane_f == i2
    onehot = jnp.where(hit1 | hit2, 1.0, 0.0)
    r_id = lax.broadcasted_iota(jnp.int32, (tm, tm), 0)
    c_id = lax.broadcasted_iota(jnp.int32, (tm, tm), 1)
    earlier = jnp.where(c_id < r_id, 1.0, 0.0).astype(BF16)
    before = jnp.dot(earlier, onehot.astype(BF16), preferred_element_type=F32) + run[0:1, :]
    rank1 = jnp.sum(jnp.where(hit1, before, 0.0), axis=-1, keepdims=True)
    rank2 = jnp.sum(jnp.where(hit2, before, 0.0), axis=-1, keepdims=True)
    total = run[0:1, :] + jnp.sum(onehot, axis=0, keepdims=True)
    run[0:1, :] = total
    cnt_ref[...] = jnp.broadcast_to(total, cnt_ref.shape)

    route = jnp.zeros_like(logits)
    for idx, val in ((ROUTE_E0, i1), (ROUTE_E1, i2), (ROUTE_W0, g_p * p1), (ROUTE_W1, g_p * p2),
                     (ROUTE_R0, rank1), (ROUTE_R1, rank2)):
        route = jnp.where(lane == idx, val, route)
    route_ref[...] = route


def _norm_route(x1, g, mod3, rw, rb, seq, shift_idx, scale_idx, n_groups, epg):
    n, d = x1.shape
    tm = _pick(TM_NORM, seq)
    tpb = seq // tm
    n_mod = mod3.shape[1]
    return pl.pallas_call(
        functools.partial(_norm_route_kernel, shift_idx=shift_idx, scale_idx=scale_idx,
                          n_groups=n_groups, epg=epg),
        out_shape=(jax.ShapeDtypeStruct((n, d // 2), jnp.uint32), jax.ShapeDtypeStruct((n, LANES), F32),
                   jax.ShapeDtypeStruct((SUBLANES, LANES), F32)),
        grid=(n // tm,),
        in_specs=[pl.BlockSpec((tm, d), lambda i: (i, 0)),
                  pl.BlockSpec((1, d), lambda i: (0, 0)),
                  pl.BlockSpec((1, n_mod, d), lambda i: (i // tpb, 0, 0)),
                  pl.BlockSpec((2, d, LANES), lambda i: (0, 0, 0)),
                  pl.BlockSpec((1, LANES), lambda i: (0, 0))],
        out_specs=(pl.BlockSpec((tm, d // 2), lambda i: (i, 0)),
                   pl.BlockSpec((tm, LANES), lambda i: (i, 0)),
                   pl.BlockSpec((SUBLANES, LANES), lambda i: (0, 0))),
        scratch_shapes=[pltpu.VMEM((SUBLANES, LANES), F32)],
        compiler_params=_params(("arbitrary",)),
        name="norm_route",
    )(x1, g.reshape(1, d), mod3, rw, rb)


FF_PARTS = 2


def _first_of_expert(be_ref, b):
    return (b == 0) | (be_ref[b] != be_ref[jnp.maximum(b - 1, 0)])


def _expert_up_math(xp, w1_ref, w3_ref, w1b, w3b, act_ref, fresh, between=None):
    @pl.when(fresh)
    def _():
        w1b[...] = w1_ref[0].astype(BF16)
        w3b[...] = w3_ref[0].astype(BF16)

    lo, hi = _unpack_pairs(xp)
    lo = lo.astype(BF16)
    hi = hi.astype(BF16)
    k = lo.shape[1]
    operands = ((lo, w1b, 0), (hi, w1b, k), (lo, w3b, 0), (hi, w3b, k))
    parts = []
    for q, (x, w, k0) in enumerate(operands):
        if between is not None:
            between(q, len(operands))
        parts.append(jnp.dot(x, w[k0:k0 + k, :], preferred_element_type=F32))
    gate = parts[0] + parts[1]
    up = parts[2] + parts[3]
    act_ref[...] = (_silu(gate) * up).astype(act_ref.dtype)


def _expert_up_gather_kernel(be_ref, nused_ref, src_ref, nval_ref, tok_ref, h_hbm, w1_ref, w3_ref,
                             act_ref, xb_ref, xbuf, sem, w1b, w3b, *, bm):
    b = pl.program_id(0)
    n_used = nused_ref[0]

    def row_copy(first, last, slot, r):
        t = tok_ref[jnp.minimum(first + r, last)]
        return pltpu.make_async_copy(h_hbm.at[pl.ds(t, 1), :], xbuf.at[slot, pl.ds(r, 1), :],
                                     sem.at[slot])

    def wait_rows(slot):
        pltpu.make_async_copy(h_hbm.at[pl.ds(0, bm), :], xbuf.at[slot], sem.at[slot]).wait()

    def block_range(blk):
        first = src_ref[jnp.minimum(blk, n_used - 1)]
        return first, first + nval_ref[jnp.minimum(blk, n_used - 1)] - 1

    @pl.when(b == 0)
    def _():
        for blk in range(GATHER_AHEAD):
            first, last = block_range(blk)

            def issue(q, carry, first=first, last=last, blk=blk):
                for u in range(DMA_UNROLL):
                    row_copy(first, last, blk, q * DMA_UNROLL + u).start()
                return carry

            lax.fori_loop(0, bm // DMA_UNROLL, issue, 0)

    @pl.when(b < n_used)
    def _():
        slot = b % GATHER_BUFS
        wait_rows(slot)
        xp = xbuf[slot]
        xb_ref[...] = xp
        first, last = block_range(b + GATHER_AHEAD)
        ahead_slot = (b + GATHER_AHEAD) % GATHER_BUFS

        def start_rows(q, n):
            for r in range(q * bm // n, (q + 1) * bm // n):
                row_copy(first, last, ahead_slot, r).start(priority=r % 2)

        _expert_up_math(xp, w1_ref, w3_ref, w1b, w3b, act_ref, _first_of_expert(be_ref, b), start_rows)

    @pl.when(b + 1 == n_used)
    def _():
        for k in range(1, GATHER_AHEAD + 1):
            wait_rows((b + k) % GATHER_BUFS)

    @pl.when(b >= n_used)
    def _():
        act_ref[...] = jnp.zeros_like(act_ref)
        xb_ref[...] = jnp.zeros_like(xb_ref)


def _expert_up_kernel(be_ref, nused_ref, xb_ref, w1_ref, w3_ref, act_ref, w1b, w3b):
    b = pl.program_id(0)

    @pl.when(b < nused_ref[0])
    def _():
        _expert_up_math(xb_ref[...], w1_ref, w3_ref, w1b, w3b, act_ref, _first_of_expert(be_ref, b))

    @pl.when(b >= nused_ref[0])
    def _():
        act_ref[...] = jnp.zeros_like(act_ref)


def _expert_up(plan, hp, w1, w3, bm):
    n_blocks = plan["block_e"].shape[0]
    assert bm % DMA_UNROLL == 0
    half = hp.shape[1]
    d = 2 * half
    f = w1.shape[2]
    assert f % FF_PARTS == 0
    fp = f // FF_PARTS
    n_rows = n_blocks * bm
    wspec = lambda part: pl.BlockSpec((1, d, fp), lambda b, be, *_: (be[b], 0, part))
    act_spec = pl.BlockSpec((bm, fp), lambda b, *_: (b, 0))
    xb_spec = pl.BlockSpec((bm, half), lambda b, *_: (b, 0))
    w_scratch = [pltpu.VMEM((d, fp), BF16), pltpu.VMEM((d, fp), BF16)]
    act_shape = jax.ShapeDtypeStruct((n_rows, fp), BF16)

    act0, xb = pl.pallas_call(
        functools.partial(_expert_up_gather_kernel, bm=bm),
        out_shape=(act_shape, jax.ShapeDtypeStruct((n_rows, half), jnp.uint32)),
        grid_spec=pltpu.PrefetchScalarGridSpec(
            num_scalar_prefetch=5,
            grid=(n_blocks,),
            in_specs=[pl.BlockSpec(memory_space=pl.ANY), wspec(0), wspec(0)],
            out_specs=(act_spec, xb_spec),
            scratch_shapes=[pltpu.VMEM((GATHER_BUFS, bm, half), jnp.uint32),
                            pltpu.SemaphoreType.DMA((GATHER_BUFS,))] + w_scratch),
        compiler_params=_params(("arbitrary",)),
        name="expert_up_gather",
    )(plan["block_e"], plan["n_used"], plan["src_start"], plan["n_valid"], plan["tok_sorted"],
      hp, w1, w3)

    acts = [act0]
    for part in range(1, FF_PARTS):
        acts.append(pl.pallas_call(
            _expert_up_kernel,
            out_shape=act_shape,
            grid_spec=pltpu.PrefetchScalarGridSpec(
                num_scalar_prefetch=2,
                grid=(n_blocks,),
                in_specs=[xb_spec, wspec(part), wspec(part)],
                out_specs=act_spec,
                scratch_shapes=w_scratch),
            compiler_params=_params(("arbitrary",)),
            name="expert_up",
        )(plan["block_e"], plan["n_used"], xb, w1, w3))
    return acts


def _expert_down_kernel(be_ref, nused_ref, *refs):
    *act_refs, w2_ref, y_ref, w2b = refs
    b = pl.program_id(0)

    @pl.when(b < nused_ref[0])
    def _():
        @pl.when(_first_of_expert(be_ref, b))
        def _():
            w2b[...] = w2_ref[0].astype(BF16)

        y = None
        k0 = 0
        for a_ref in act_refs:
            k1 = k0 + a_ref.shape[1]
            part = jnp.dot(a_ref[...], w2b[k0:k1, :], preferred_element_type=F32)
            y = part if y is None else y + part
            k0 = k1
        y_ref[...] = _pack_pairs(y)

    @pl.when(b >= nused_ref[0])
    def _():
        y_ref[...] = jnp.zeros_like(y_ref)


def _expert_down(plan, acts, w2, bm):
    n_blocks = plan["block_e"].shape[0]
    f, d = w2.shape[1], w2.shape[2]
    act_specs = [pl.BlockSpec((bm, a.shape[1]), lambda b, be, nu: (b, 0)) for a in acts]
    grid_spec = pltpu.PrefetchScalarGridSpec(
        num_scalar_prefetch=2,
        grid=(n_blocks,),
        in_specs=act_specs + [pl.BlockSpec((1, f, d), lambda b, be, nu: (be[b], 0, 0))],
        out_specs=pl.BlockSpec((bm, d // 2), lambda b, be, nu: (b, 0)),
        scratch_shapes=[pltpu.VMEM((f, d), BF16)],
    )
    return pl.pallas_call(
        _expert_down_kernel,
        out_shape=jax.ShapeDtypeStruct((n_blocks * bm, d // 2), jnp.uint32),
        grid_spec=grid_spec,
        compiler_params=_params(("arbitrary",)),
        name="expert_down",
    )(plan["block_e"], plan["n_used"], *acts, w2)


def _combine_kernel(d0_ref, d1_ref, y_hbm, x_ref, route_ref, mod_ref, g_ref, o_ref,
                    ybuf, sem, *, tm, gate_idx):
    i = pl.program_id(0)
    n_tiles = pl.num_programs(0)

    def start_rows(tile, slot, rows):
        for r in rows:
            t = tile * tm + r
            for k, d_ref in enumerate((d0_ref, d1_ref)):
                pltpu.make_async_copy(y_hbm.at[pl.ds(d_ref[t], 1), :],
                                      ybuf.at[slot, k, pl.ds(r, 1), :], sem.at[slot, k]).start()

    def wait_rows(slot):
        for k in range(TOP_K):
            pltpu.make_async_copy(y_hbm.at[pl.ds(0, tm), :], ybuf.at[slot, k], sem.at[slot, k]).wait()

    @pl.when(i == 0)
    def _():
        for tile in range(GATHER_AHEAD):
            def issue(q, carry, tile=tile):
                start_rows(jnp.minimum(tile, n_tiles - 1), tile,
                           [q * DMA_UNROLL + u for u in range(DMA_UNROLL)])
                return carry

            lax.fori_loop(0, tm // DMA_UNROLL, issue, 0)

    slot = i % GATHER_BUFS
    wait_rows(slot)
    ahead = jnp.minimum(i + GATHER_AHEAD, n_tiles - 1)
    ahead_slot = (i + GATHER_AHEAD) % GATHER_BUFS
    half = ybuf.shape[-1]
    grp = tm // COMBINE_GROUPS
    for q in range(COMBINE_GROUPS):
        rows = slice(q * grp, (q + 1) * grp)
        start_rows(ahead, ahead_slot, range(q * grp, (q + 1) * grp))
        route = route_ref[rows, :]
        w0 = route[:, ROUTE_W0:ROUTE_W0 + 1]
        w1 = route[:, ROUTE_W1:ROUTE_W1 + 1]
        halves = zip(_unpack_pairs(ybuf[slot, 0, rows, :]), _unpack_pairs(ybuf[slot, 1, rows, :]),
                     (slice(0, half), slice(half, 2 * half)))
        xs = [x_ref[rows, cs] + mod_ref[0, gate_idx:gate_idx + 1, cs] * (w0 * y0 + w1 * y1)
              for y0, y1, cs in halves]
        ms = sum(jnp.sum(x * x, axis=-1, keepdims=True) for x in xs) / (2 * half)
        inv = lax.rsqrt(ms + RMS_EPS)
        o_ref[rows, 0:half] = xs[0] * inv * g_ref[:, 0:half]
        o_ref[rows, half:] = xs[1] * inv * g_ref[:, half:]

    @pl.when(i + 1 == n_tiles)
    def _():
        for k in range(1, GATHER_AHEAD + 1):
            wait_rows((i + k) % GATHER_BUFS)


def _combine(d0, d1, yb, x1, route, mod3, g, seq, gate_idx):
    n, d = x1.shape
    tm = _pick(TM_COMBINE, seq)
    assert tm % DMA_UNROLL == 0
    tpb = seq // tm
    n_mod = mod3.shape[1]
    grid_spec = pltpu.PrefetchScalarGridSpec(
        num_scalar_prefetch=2,
        grid=(n // tm,),
        in_specs=[pl.BlockSpec(memory_space=pl.ANY),
                  pl.BlockSpec((tm, d), lambda i, a, b: (i, 0)),
                  pl.BlockSpec((tm, LANES), lambda i, a, b: (i, 0)),
                  pl.BlockSpec((1, n_mod, d), lambda i, a, b: (i // tpb, 0, 0)),
                  pl.BlockSpec((1, d), lambda i, a, b: (0, 0))],
        out_specs=pl.BlockSpec((tm, d), lambda i, a, b: (i, 0)),
        scratch_shapes=[pltpu.VMEM((GATHER_BUFS, TOP_K, tm, d // 2), jnp.uint32),
                        pltpu.SemaphoreType.DMA((GATHER_BUFS, TOP_K))],
    )
    return pl.pallas_call(
        functools.partial(_combine_kernel, tm=tm, gate_idx=gate_idx),
        out_shape=jax.ShapeDtypeStruct((n, d), F32),
        grid_spec=grid_spec,
        compiler_params=_params(("arbitrary",)),
        name="combine",
    )(d0, d1, yb, x1, route, mod3, g.reshape(1, d))


def _dispatch_plan(route, counts, n_exp, bm):
    n_tok = route.shape[0]
    n_assign = n_tok * TOP_K
    i32 = jnp.int32
    e0 = route[:, ROUTE_E0].astype(i32)
    e1 = route[:, ROUTE_E1].astype(i32)
    counts = counts.astype(i32)
    starts = jnp.cumsum(counts) - counts
    padded = ((counts + bm - 1) // bm) * bm
    pad_ends = jnp.cumsum(padded)
    pad_starts = pad_ends - padded
    eids = jnp.arange(n_exp, dtype=i32)

    def table(tab, idx):
        return jnp.sum(jnp.where(idx[:, None] == eids[None, :], tab[None, :], 0), axis=1)

    d0 = table(pad_starts, e0) + route[:, ROUTE_R0].astype(i32)
    d1 = table(pad_starts, e1) + route[:, ROUTE_R1].astype(i32)
    flat_e = jnp.stack([e0, e1], axis=1).reshape(n_assign)
    key = flat_e * n_assign + jnp.arange(n_assign, dtype=i32)
    tok_sorted = (jnp.sort(key) % n_assign) // TOP_K
    n_blocks = -(-n_assign // bm) + n_exp
    n_used = pad_ends[-1] // bm
    blk = jnp.minimum(jnp.arange(n_blocks, dtype=i32), n_used - 1)
    block_e = jnp.sum((blk[:, None] * bm >= pad_ends[None, :]).astype(i32), axis=1)
    k = blk - table(pad_starts, block_e) // bm
    src_start = table(starts, block_e) + k * bm
    n_valid = jnp.clip(table(counts, block_e) - k * bm, 1, bm)
    plan = dict(block_e=block_e, n_used=n_used.reshape(1), src_start=src_start, n_valid=n_valid,
                tok_sorted=tok_sorted)
    return plan, d0, d1


def kernel(x, c, ada_w, ada_b, norm_mix_g, w_in, b_in, rnn_conv_w, rnn_conv_b, lru_wa, lru_ba, lru_wx, lru_bx, lru_lambda, w_rnn_out, conf_dw_w, conf_dw_b, conf_ln_g, conf_ln_b, w_conv_out, w_o, norm_ffn_g, router_group_w, router_group_b, router_expert_w, router_expert_b, expert_w1, expert_w3, expert_w2, final_norm_g):
    bsz, seq, d = x.shape
    depth = ada_w.shape[0]
    n_mod = ada_w.shape[2] // d
    d_rnn = lru_lambda.shape[1]
    d_conv = conf_dw_b.shape[1]
    n_groups = router_group_w.shape[2]
    n_exp = router_expert_w.shape[2]
    epg = n_exp // n_groups
    assert depth == 1 and d_rnn == d and d_conv == d and n_exp + n_groups <= LANES
    n_tok = bsz * seq
    x2d = x.reshape(n_tok, d)
    l = 0

    mod = _ada_mod(c, ada_w[l], ada_b[l])
    mod3 = mod.reshape(bsz, n_mod, d)
    w_in_b = _cast_bf16(w_in[l])
    b_in2 = b_in[l].reshape(1, -1)

    h = _norm_mod(x2d, norm_mix_g[l], mod3, seq, 0, 1)
    rnn_g, sa = _rnn_branch(h, w_in_b, b_in2, rnn_conv_w[l], rnn_conv_b[l],
                        lru_wa[l].astype(BF16), lru_ba[l], lru_wx[l].astype(BF16), lru_bx[l],
                        lru_lambda[l], seq, d_rnn, 4)
    uc, sb = _conv_branch(h, w_in_b, b_in2, conf_dw_w[l], conf_dw_b[l], seq, d_conv, 2, 5)
    merged = _merge(rnn_g, uc, sa, sb, w_rnn_out[l].astype(BF16), w_conv_out[l].astype(BF16),
                    conf_ln_g[l], conf_ln_b[l])
    x1 = _out_proj(merged, w_o[l].astype(BF16), x2d, mod3, seq, 2)

    rw = jnp.zeros((d, LANES), F32).at[:, :n_exp].set(router_expert_w[l])
    rw = rw.at[:, n_exp:n_exp + n_groups].set(router_group_w[l])
    rw_hi = rw.astype(BF16)
    rw = jnp.stack([rw_hi, (rw - rw_hi.astype(F32)).astype(BF16)])
    rb = jnp.zeros((1, LANES), F32).at[0, :n_exp].set(router_expert_b[l])
    rb = rb.at[0, n_exp:n_exp + n_groups].set(router_group_b[l])
    h2, route, counts = _norm_route(x1, norm_ffn_g[l], mod3, rw, rb, seq, 3, 4, n_groups, epg)

    bm = BM_EXPERT
    plan, d0, d1 = _dispatch_plan(route, counts[0, :n_exp], n_exp, bm)
    acts = _expert_up(plan, h2, expert_w1[l], expert_w3[l], bm)
    yb = _expert_down(plan, acts, expert_w2[l], bm)
    out = _combine(d0, d1, yb, x1, route, mod3, final_norm_g, seq, 5)
    return out.reshape(bsz, seq, d)
```
